```python
import jax
import jax.numpy as jnp
from jax import lax
import numpy as np

D_MODEL = 1024
BATCH = 32
SEQ = 2048
DEPTH = 4

N_MIXERS = 4
GROUP_WIDTH = D_MODEL // N_MIXERS
HEADS = 4
HEAD_DIM = GROUP_WIDTH // HEADS
CONV_WIDTH = 3
SGU_CHUNK = 128
GLA_KEY_DIM = HEAD_DIM // 2
GLA_GATE_RANK = 16
GLA_GATE_NORMALIZER = 16.0
LA_CHUNK = 16
D_FF = 2816
N_EXPERTS = 8
TOP_K = 2
D_FF_EXPERT = 3584
N_MOD = 6
N_DENSE_LAYERS = (DEPTH + 1) // 2
N_MOE_LAYERS = DEPTH // 2
EPS = 1e-6
PROJ_SIZES = (
    GROUP_WIDTH, GROUP_WIDTH, GROUP_WIDTH,
    GROUP_WIDTH, GROUP_WIDTH,
    HEADS * GLA_KEY_DIM, HEADS * GLA_KEY_DIM, GROUP_WIDTH,
    GLA_GATE_RANK, GROUP_WIDTH,
    GROUP_WIDTH, GROUP_WIDTH, GROUP_WIDTH, GROUP_WIDTH,
)
IN_PROJ_WIDTH = sum(PROJ_SIZES)

kernel_name = "hybrid_parallel_groups_adaln_moe_trunk"


def rms_norm(x, g):
    xf = x.astype(jnp.float32)
    y = xf * lax.rsqrt(jnp.mean(xf * xf, axis=-1, keepdims=True) + EPS)
    return y.astype(x.dtype) * g


def causal_depthwise_conv(x, w):
    s = x.shape[1]
    xp = jnp.pad(x, ((0, 0), (CONV_WIDTH - 1, 0), (0, 0)))
    return sum(xp[:, k:k + s] * w[k] for k in range(CONV_WIDTH))


def short_conv_mixer(b_gate, c_gate, x_in, w_conv):
    return b_gate * causal_depthwise_conv(c_gate * x_in, w_conv)


def spatial_gating_mixer(u, v, norm_g, w_s, b_s):
    bn, s, _ = v.shape
    n = s // SGU_CHUNK
    vf = v.astype(jnp.float32)
    mu = jnp.mean(vf, axis=-1, keepdims=True)
    var = jnp.mean(jnp.square(vf - mu), axis=-1, keepdims=True)
    vn = ((vf - mu) * lax.rsqrt(var + EPS)).astype(v.dtype) * norm_g
    vc = vn.reshape(bn, n, SGU_CHUNK, HEADS, HEAD_DIM)
    causal = jnp.tril(jnp.ones((SGU_CHUNK, SGU_CHUNK), dtype=bool))
    ws = jnp.where(causal[None], w_s, 0.0)
    mixed = jnp.einsum('hts,bnshd->bnthd', ws, vc) + b_s.T[None, None, :, :, None]
    return u * mixed.reshape(bn, s, GROUP_WIDTH)


def chunk_gated_linear_attention(q, k, v, log_a):
    bn, s, h, kd = q.shape
    vd = v.shape[-1]
    n = s // LA_CHUNK
    f32 = jnp.float32
    q, k, v, log_a = (t.astype(f32).reshape(bn, n, LA_CHUNK, h, t.shape[-1]) for t in (q, k, v, log_a))
    b = jnp.cumsum(log_a, axis=2)
    mid = LA_CHUNK // 2
    b_ref = b[:, :, mid:mid + 1]
    scores = jnp.einsum('bnihk,bnjhk->bnhij', q * jnp.exp(b - b_ref), k * jnp.exp(b_ref - b))
    causal = jnp.tril(jnp.ones((LA_CHUNK, LA_CHUNK), dtype=bool))
    scores = jnp.where(causal, scores, 0.0)
    o_intra = jnp.einsum('bnhij,bnjhv->bnihv', scores, v)
    b_last = b[:, :, -1:]
    k_upd = k * jnp.exp(b_last - b)
    decay = jnp.exp(b_last[:, :, 0])

    def step(state, xs):
        d, kc, vc = xs
        new_state = d[..., None] * state + jnp.einsum('bjhk,bjhv->bhkv', kc, vc)
        return new_state, state

    s0 = jnp.zeros((bn, h, kd, vd), f32)
    _, s_prev = lax.scan(step, s0, (jnp.moveaxis(decay, 1, 0), jnp.moveaxis(k_upd, 1, 0), jnp.moveaxis(v, 1, 0)))
    o_inter = jnp.einsum('bnihk,nbhkv->bnihv', q * jnp.exp(b), s_prev)
    return (o_intra + o_inter).reshape(bn, s, h, vd)


def gla_mixer(q, k, v, g_lowrank, r, w_gate, b_gate, norm_g):
    bn, s, _ = q.shape
    log_a = jax.nn.log_sigmoid((g_lowrank @ w_gate + b_gate).astype(jnp.float32)) / GLA_GATE_NORMALIZER
    o = chunk_gated_linear_attention(
        q.reshape(bn, s, HEADS, GLA_KEY_DIM) * (GLA_KEY_DIM ** -0.5),
        k.reshape(bn, s, HEADS, GLA_KEY_DIM),
        v.reshape(bn, s, HEADS, HEAD_DIM),
        log_a.reshape(bn, s, HEADS, GLA_KEY_DIM))
    o = rms_norm(o.astype(q.dtype), norm_g)
    return o.reshape(bn, s, GROUP_WIDTH) * jax.nn.silu(r)


def hgrn2_mixer(q, f_logit, i_in, g, lower_bound, norm_g):
    bn, s, _ = q.shape
    f = lower_bound + (1.0 - lower_bound) * jax.nn.sigmoid(f_logit.astype(jnp.float32))
    o = chunk_gated_linear_attention(
        q.reshape(bn, s, HEADS, HEAD_DIM),
        (1.0 - f).reshape(bn, s, HEADS, HEAD_DIM),
        i_in.reshape(bn, s, HEADS, HEAD_DIM),
        jnp.log(f).reshape(bn, s, HEADS, HEAD_DIM))
    o = rms_norm(o.astype(q.dtype), norm_g)
    return o.reshape(bn, s, GROUP_WIDTH) * jax.nn.silu(g)


def swiglu(x, w1, w3, w2):
    return (jax.nn.silu(x @ w1) * (x @ w3)) @ w2


def moe_swiglu(x, w_router, w1, w3, w2):
    bn, s, d = x.shape
    t = x.reshape(bn * s, d)
    logits = (t @ w_router).astype(jnp.float32)
    top_val, top_idx = lax.top_k(logits, TOP_K)
    top_w = jax.nn.softmax(top_val, axis=-1)
    combine = jnp.sum(jax.nn.one_hot(top_idx, N_EXPERTS, dtype=jnp.float32) * top_w[..., None], axis=1)
    out = jnp.zeros_like(t)
    for e in range(N_EXPERTS):
        out = out + combine[:, e:e + 1].astype(t.dtype) * swiglu(t, w1[e], w3[e], w2[e])
    return out.reshape(bn, s, d)


def setup_inputs(seed: int = 0) -> dict:
    key = jax.random.key(seed)
    ks = jax.random.split(key, 25)
    f32 = jnp.float32

    def normal(k, shape, scale):
        return scale * jax.random.normal(k, shape, f32)

    def gain(k, shape):
        return 1.0 + normal(k, shape, 0.02)

    L = DEPTH
    return {
        "x": normal(ks[0], (BATCH, SEQ, D_MODEL), 1.0),
        "c": normal(ks[1], (BATCH, D_MODEL), 1.0),
        "norm_mix_g": gain(ks[2], (L, D_MODEL)),
        "norm_ffn_g": gain(ks[3], (L, D_MODEL)),
        "final_norm_g": gain(ks[4], (D_MODEL,)),
        "w_ada": normal(ks[5], (L, D_MODEL, N_MOD * D_MODEL), 0.5 * D_MODEL ** -0.5),
        "b_ada": normal(ks[6], (L, N_MOD * D_MODEL), 0.02),
        "w_in": normal(ks[7], (L, D_MODEL, IN_PROJ_WIDTH), D_MODEL ** -0.5),
        "w_out": normal(ks[8], (L, N_MIXERS * GROUP_WIDTH, D_MODEL), (N_MIXERS * GROUP_WIDTH) ** -0.5),
        "conv_w": normal(ks[9], (L, CONV_WIDTH, GROUP_WIDTH), CONV_WIDTH ** -0.5),
        "sgu_norm_g": gain(ks[10], (L, GROUP_WIDTH)),
        "sgu_w": normal(ks[11], (L, HEADS, SGU_CHUNK, SGU_CHUNK), SGU_CHUNK ** -0.5),
        "sgu_b": gain(ks[12], (L, HEADS, SGU_CHUNK)),
        "gla_w_gate": normal(ks[13], (L, GLA_GATE_RANK, HEADS * GLA_KEY_DIM), GLA_GATE_RANK ** -0.5),
        "gla_b_gate": normal(ks[14], (L, HEADS * GLA_KEY_DIM), 0.1),
        "gla_norm_g": gain(ks[15], (L, HEAD_DIM)),
        "hgrn_lower_bounds": normal(ks[16], (L, GROUP_WIDTH), 0.1),
        "hgrn_norm_g": gain(ks[17], (L, HEAD_DIM)),
        "ffn_w1": normal(ks[18], (N_DENSE_LAYERS, D_MODEL, D_FF), D_MODEL ** -0.5),
        "ffn_w3": normal(ks[19], (N_DENSE_LAYERS, D_MODEL, D_FF), D_MODEL ** -0.5),
        "ffn_w2": normal(ks[20], (N_DENSE_LAYERS, D_FF, D_MODEL), D_FF ** -0.5),
        "moe_router": normal(ks[21], (N_MOE_LAYERS, D_MODEL, N_EXPERTS), D_MODEL ** -0.5),
        "moe_w1": normal(ks[22], (N_MOE_LAYERS, N_EXPERTS, D_MODEL, D_FF_EXPERT), D_MODEL ** -0.5),
        "moe_w3": normal(ks[23], (N_MOE_LAYERS, N_EXPERTS, D_MODEL, D_FF_EXPERT), D_MODEL ** -0.5),
        "moe_w2": normal(ks[24], (N_MOE_LAYERS, N_EXPERTS, D_FF_EXPERT, D_MODEL), D_FF_EXPERT ** -0.5),
    }


def reference(x, c, norm_mix_g, norm_ffn_g, final_norm_g, w_ada, b_ada, w_in, w_out, conv_w,
              sgu_norm_g, sgu_w, sgu_b, gla_w_gate, gla_b_gate, gla_norm_g, hgrn_lower_bounds,
              hgrn_norm_g, ffn_w1, ffn_w3, ffn_w2, moe_router, moe_w1, moe_w3, moe_w2):
    split_points = np.cumsum(PROJ_SIZES)[:-1].tolist()
    cond = jax.nn.silu(c)
    lb_cum = jnp.cumsum(jax.nn.softmax(hgrn_lower_bounds.astype(jnp.float32), axis=0), axis=0)
    lower_bounds = lb_cum - lb_cum[0]
    for layer in range(DEPTH):
        mod = cond @ w_ada[layer] + b_ada[layer]
        shift_m, scale_m, gate_m, shift_f, scale_f, gate_f = [m[:, None, :] for m in jnp.split(mod, N_MOD, axis=-1)]
        h = rms_norm(x, norm_mix_g[layer]) * (1.0 + scale_m) + shift_m
        (cb, cc, cx, su, sv, aq, ak, av, ag, ar, hq, hf, hi, hg) = jnp.split(h @ w_in[layer], split_points, axis=-1)
        y_conv = short_conv_mixer(cb, cc, cx, conv_w[layer])
        y_sgu = spatial_gating_mixer(su, sv, sgu_norm_g[layer], sgu_w[layer], sgu_b[layer])
        y_gla = gla_mixer(aq, ak, av, ag, ar, gla_w_gate[layer], gla_b_gate[layer], gla_norm_g[layer])
        y_hgrn = hgrn2_mixer(hq, hf, hi, hg, lower_bounds[layer], hgrn_norm_g[layer])
        mixed = jnp.concatenate([y_conv, y_sgu, y_gla, y_hgrn], axis=-1) @ w_out[layer]
        x = x + gate_m * mixed
        h = rms_norm(x, norm_ffn_g[layer]) * (1.0 + scale_f) + shift_f
        idx = layer // 2
        if layer % 2 == 0:
            y = swiglu(h, ffn_w1[idx], ffn_w3[idx], ffn_w2[idx])
        else:
            y = moe_swiglu(h, moe_router[idx], moe_w1[idx], moe_w3[idx], moe_w2[idx])
        x = x + gate_f * y
    return rms_norm(x, final_norm_g)
```

```python
import functools

import jax
import jax.numpy as jnp
from jax import lax
from jax.experimental import pallas as pl
from jax.experimental.pallas import tpu as pltpu

F32 = jnp.float32
BF16 = jnp.bfloat16
I32 = jnp.int32
HI = lax.Precision.HIGHEST
EPS = 1e-6

D_MODEL = 1024
N_MOD = 6
GROUP_WIDTH = 256
HEADS = 4
HEAD_DIM = 64
GLA_KEY_DIM = 32
GLA_GATE_RANK = 16
GLA_GATE_NORMALIZER = 16.0
LA_CHUNK_LOG2 = 4
SUPER = 128
SUPER_LOG2 = 7
N_EXPERTS = 8
LANES = 128
BF16_ROWS = 16

C_CB, C_CC, C_CX, C_SU, C_SV = 0, 256, 512, 768, 1024
C_AQ, C_AK, C_AV, C_AG, C_AR = 1280, 1408, 1536, 1792, 1920
C_HQ, C_HF, C_HI, C_HG = 2176, 2432, 2688, 2944
IN_PROJ_PAD = 3200
IN_PROJ_SPLIT = 1808

VMEM_LIMIT = 56 * 1024 * 1024


def _silu(x):
    return x * jax.nn.sigmoid(x)


def _dot(a, b):
    return jnp.dot(a, b, preferred_element_type=F32)


def _dot_nt(a, b):
    return lax.dot_general(a, b, (((1,), (1,)), ((), ())), preferred_element_type=F32)


def _dot_tn(a, b):
    return lax.dot_general(a, b, (((0,), (0,)), ((), ())), preferred_element_type=F32)


def _mod_norm(xt, g, shift, scale):
    ms = jnp.mean(xt * xt, axis=-1, keepdims=True)
    return (xt * lax.rsqrt(ms + EPS)) * g * (1.0 + scale) + shift


def _ada_kernel(c_ref, w_ref, b_ref, o_ref):
    cond = _silu(c_ref[...])
    o_ref[0] = _dot(cond.astype(BF16), w_ref[0].astype(BF16)) + b_ref[0]


def _ada_call(c, w_ada, b_ada):
    depth, d, n = w_ada.shape
    bsz = c.shape[0]
    tn = 1536
    return pl.pallas_call(
        _ada_kernel,
        grid=(depth, n // tn),
        in_specs=[pl.BlockSpec((bsz, d), lambda l, j: (0, 0)),
                  pl.BlockSpec((1, d, tn), lambda l, j: (l, 0, j)),
                  pl.BlockSpec((1, 1, tn), lambda l, j: (l, 0, j))],
        out_specs=pl.BlockSpec((1, bsz, tn), lambda l, j: (l, 0, j)),
        out_shape=jax.ShapeDtypeStruct((depth, bsz, n), F32),
        compiler_params=pltpu.CompilerParams(dimension_semantics=("arbitrary", "arbitrary"),
                                             vmem_limit_bytes=VMEM_LIMIT),
        name="adaln_mod",
    )(c, w_ada, b_ada.reshape(depth, 1, n))


def _pair_masks(rows):
    t = lax.broadcasted_iota(I32, (rows, SUPER), 0) & (SUPER - 1)
    s = lax.broadcasted_iota(I32, (rows, SUPER), 1)
    masks = [((t >> LA_CHUNK_LOG2) == (s >> LA_CHUNK_LOG2)) & (s <= t)]
    for sh in range(LA_CHUNK_LOG2 + 1, SUPER_LOG2 + 1):
        half = 1 << (sh - 1)
        masks.append(((t >> sh) == (s >> sh)) & ((t & half) != 0) & ((s & half) == 0))
    return masks


def _level_refs(c):
    kt = c.shape[-1]
    n16 = SUPER >> LA_CHUNK_LOG2
    mid = (1 << LA_CHUNK_LOG2) // 2
    refs = [jnp.broadcast_to(c.reshape(n16, 1 << LA_CHUNK_LOG2, kt)[:, mid:mid + 1, :],
                             (n16, 1 << LA_CHUNK_LOG2, kt)).reshape(SUPER, kt)]
    for sh in range(LA_CHUNK_LOG2 + 1, SUPER_LOG2 + 1):
        nb, bs, half = SUPER >> sh, 1 << sh, 1 << (sh - 1)
        refs.append(jnp.broadcast_to(c.reshape(nb, bs, kt)[:, half - 1:half, :], (nb, bs, kt)).reshape(SUPER, kt))
    return refs


def _gated_linear_attention_step(q, k, v, log_a, st_ref, ltri, masks):
    kt = q.shape[-1]
    hk = kt // HEADS
    c = jnp.dot(ltri, log_a, precision=HI, preferred_element_type=F32)
    klane_head = lax.broadcasted_iota(I32, (SUPER, kt), 1) // hk
    vlane_head = lax.broadcasted_iota(I32, (SUPER, GROUP_WIDTH), 1) // HEAD_DIM

    scores = None
    for lvl, (cref, mask) in enumerate(zip(_level_refs(c), masks)):
        dq, dk = c - cref, cref - c
        if lvl > 0:
            dq, dk = jnp.minimum(dq, 0.0), jnp.minimum(dk, 0.0)
        qe = q * jnp.exp(dq)
        ke = (k * jnp.exp(dk)).astype(BF16)
        q_heads = jnp.concatenate([jnp.where(klane_head == h, qe, 0.0).astype(BF16) for h in range(HEADS)], axis=0)
        s_lvl = _dot_nt(q_heads, ke)
        scores = jnp.where(mask, s_lvl, 0.0 if scores is None else scores)
    o_heads = _dot(scores.astype(BF16), v.astype(BF16))
    o = None
    for h in range(HEADS):
        part = jnp.where(vlane_head == h, o_heads[h * SUPER:(h + 1) * SUPER], 0.0)
        o = part if o is None else o + part

    st = st_ref[...]
    o = o + _dot_nt((q * jnp.exp(c)).astype(BF16), st.astype(BF16))
    c_last = c[SUPER - 1:SUPER, :]
    kd = (k * jnp.exp(c_last - c)).astype(BF16)
    upd = _dot_tn(v.astype(BF16), kd)
    diag = (lax.broadcasted_iota(I32, (GROUP_WIDTH, kt), 0) // HEAD_DIM) == (
        lax.broadcasted_iota(I32, (GROUP_WIDTH, kt), 1) // hk)
    st_ref[...] = st * jnp.exp(c_last) + jnp.where(diag, upd, 0.0)
    return o


def _head_rms_gate(o, head_mean, g, gate):
    ms = _dot((o * o).astype(BF16), head_mean)
    return (o * lax.rsqrt(ms + EPS)) * g * _silu(gate)


def _mixer_kernel(layer, ts, x_ref, mod_ref, ng_ref, win_ref, wout_ref, convw_ref, sgug_ref, sguw_ref, sgub_ref,
                  wgate_ref, bgate_ref, glag_ref, hlb_ref, hgng_ref, o_ref,
                  p_ref, y_ref, zbuf, st_gla, st_hgrn):
    @pl.when(pl.program_id(1) == 0)
    def _():
        zbuf[0:8, :] = jnp.zeros((8, GROUP_WIDTH), F32)
        st_gla[...] = jnp.zeros_like(st_gla)
        st_hgrn[...] = jnp.zeros_like(st_hgrn)

    xt = x_ref[0]
    mod = mod_ref[...]
    h = _mod_norm(xt, ng_ref[...], mod[0:1], mod[1:2])
    p_ref[...] = _dot(h.astype(BF16), win_ref[...])

    z = p_ref[:, C_CC:C_CC + GROUP_WIDTH] * p_ref[:, C_CX:C_CX + GROUP_WIDTH]
    zbuf[8:8 + ts, :] = z
    cw = convw_ref[...]
    conv = cw[0:1] * zbuf[6:6 + ts, :] + cw[1:2] * zbuf[7:7 + ts, :] + cw[2:3] * z
    y_ref[:, 0:GROUP_WIDTH] = (p_ref[:, C_CB:C_CB + GROUP_WIDTH] * conv).astype(BF16)
    zbuf[0:8, :] = zbuf[ts:ts + 8, :]

    masks = _pair_masks(HEADS * SUPER)
    ri = lax.broadcasted_iota(I32, (SUPER, SUPER), 0)
    ci = lax.broadcasted_iota(I32, (SUPER, SUPER), 1)
    ltri = jnp.where(ci <= ri, 1.0, 0.0).astype(F32)
    t_st = lax.broadcasted_iota(I32, (HEADS * SUPER, SUPER), 0) & (SUPER - 1)
    s_st = lax.broadcasted_iota(I32, (HEADS * SUPER, SUPER), 1)
    ws = jnp.where(s_st <= t_st, sguw_ref[...], 0.0).astype(BF16)
    vlane_head = lax.broadcasted_iota(I32, (SUPER, GROUP_WIDTH), 1) // HEAD_DIM
    hr = lax.broadcasted_iota(I32, (GROUP_WIDTH, GROUP_WIDTH), 0) // HEAD_DIM
    hc = lax.broadcasted_iota(I32, (GROUP_WIDTH, GROUP_WIDTH), 1) // HEAD_DIM
    head_mean = jnp.where(hr == hc, 1.0 / HEAD_DIM, 0.0).astype(BF16)
    lb_all = hlb_ref[...]
    lb_e = jnp.exp(lb_all - jnp.max(lb_all, axis=0, keepdims=True))
    lb_sm = lb_e / jnp.sum(lb_e, axis=0, keepdims=True)
    lb = jnp.zeros((1, GROUP_WIDTH), F32)
    for j in range(1, layer + 1):
        lb = lb + lb_sm[j:j + 1]

    for i in range(ts // SUPER):
        r = slice(i * SUPER, (i + 1) * SUPER)
        sv = p_ref[r, C_SV:C_SV + GROUP_WIDTH]
        mu = jnp.mean(sv, axis=-1, keepdims=True)
        dv = sv - mu
        var = jnp.mean(dv * dv, axis=-1, keepdims=True)
        vn = (dv * lax.rsqrt(var + EPS)) * sgug_ref[...]
        mix_heads = _dot(ws, vn.astype(BF16))
        mixed = sgub_ref[...]
        for hh in range(HEADS):
            mixed = mixed + jnp.where(vlane_head == hh, mix_heads[hh * SUPER:(hh + 1) * SUPER], 0.0)
        y_ref[r, GROUP_WIDTH:2 * GROUP_WIDTH] = (p_ref[r, C_SU:C_SU + GROUP_WIDTH] * mixed).astype(BF16)

        gate_logit = _dot(p_ref[r, C_AG:C_AG + LANES].astype(BF16), wgate_ref[...]) + bgate_ref[...]
        log_a = (jnp.minimum(gate_logit, 0.0) - jnp.log1p(jnp.exp(-jnp.abs(gate_logit)))) * (1.0 / GLA_GATE_NORMALIZER)
        o = _gated_linear_attention_step(
            p_ref[r, C_AQ:C_AQ + LANES] * (GLA_KEY_DIM ** -0.5), p_ref[r, C_AK:C_AK + LANES],
            p_ref[r, C_AV:C_AV + GROUP_WIDTH], log_a, st_gla, ltri, masks)
        y_ref[r, 2 * GROUP_WIDTH:3 * GROUP_WIDTH] = _head_rms_gate(
            o, head_mean, glag_ref[...], p_ref[r, C_AR:C_AR + GROUP_WIDTH]).astype(BF16)

        f = lb + (1.0 - lb) * jax.nn.sigmoid(p_ref[r, C_HF:C_HF + GROUP_WIDTH])
        o = _gated_linear_attention_step(
            p_ref[r, C_HQ:C_HQ + GROUP_WIDTH], 1.0 - f, p_ref[r, C_HI:C_HI + GROUP_WIDTH], jnp.log(f),
            st_hgrn, ltri, masks)
        y_ref[r, 3 * GROUP_WIDTH:4 * GROUP_WIDTH] = _head_rms_gate(
            o, head_mean, hgng_ref[...], p_ref[r, C_HG:C_HG + GROUP_WIDTH]).astype(BF16)

    o_ref[0] = xt + mod[2:3] * _dot(y_ref[...], wout_ref[...])


def _const_spec(shape, nargs):
    zeros = (0,) * len(shape)
    if nargs == 2:
        return pl.BlockSpec(shape, lambda a, b: zeros, pipeline_mode=pl.Buffered(1))
    return pl.BlockSpec(shape, lambda a, b, *_: zeros, pipeline_mode=pl.Buffered(1))


def _mixer_call(layer, x, mod_l, ng, win, wout, convw, sgug, sguw, sgub, wgate, bgate, glag, hlb, hgng):
    bsz, seq, d = x.shape
    ts = 256
    small = [ng, win, wout, convw, sgug, sguw, sgub, wgate, bgate, glag, hlb, hgng]
    return pl.pallas_call(
        functools.partial(_mixer_kernel, layer, ts),
        grid=(bsz, seq // ts),
        in_specs=[pl.BlockSpec((1, ts, d), lambda b, s: (b, s, 0)),
                  pl.BlockSpec((None, N_MOD, d), lambda b, s: (b, 0, 0))]
                 + [_const_spec(a.shape, 2) for a in small],
        out_specs=pl.BlockSpec((1, ts, d), lambda b, s: (b, s, 0)),
        out_shape=jax.ShapeDtypeStruct(x.shape, F32),
        scratch_shapes=[pltpu.VMEM((ts, IN_PROJ_PAD), F32),
                        pltpu.VMEM((ts, d), BF16),
                        pltpu.VMEM((ts + 8, GROUP_WIDTH), F32),
                        pltpu.VMEM((GROUP_WIDTH, HEADS * GLA_KEY_DIM), F32),
                        pltpu.VMEM((GROUP_WIDTH, GROUP_WIDTH), F32)],
        input_output_aliases={0: 0},
        compiler_params=pltpu.CompilerParams(dimension_semantics=("parallel", "arbitrary"),
                                             vmem_limit_bytes=VMEM_LIMIT),
        name=f"mixer_l{layer}",
    )(x, mod_l, *small)


def _ffn_kernel(final, x_ref, mod_ref, ng_ref, w1_ref, w3_ref, w2_ref, fng_ref, o_ref):
    xt = x_ref[0]
    mod = mod_ref[...]
    hb = _mod_norm(xt, ng_ref[...], mod[3:4], mod[4:5]).astype(BF16)
    h1 = _dot(hb, w1_ref[...])
    h3 = _dot(hb, w3_ref[...])
    a = (_silu(h1) * h3).astype(BF16)
    out = xt + mod[5:6] * _dot(a, w2_ref[...])
    if final:
        ms = jnp.mean(out * out, axis=-1, keepdims=True)
        out = (out * lax.rsqrt(ms + EPS)) * fng_ref[...]
    o_ref[0] = out


def _ffn_call(final, x, mod_l, ng, w1, w3, w2, fng):
    bsz, seq, d = x.shape
    tm = 512
    small = [ng, w1, w3, w2, fng]
    return pl.pallas_call(
        functools.partial(_ffn_kernel, final),
        grid=(bsz, seq // tm),
        in_specs=[pl.BlockSpec((1, tm, d), lambda b, s: (b, s, 0)),
                  pl.BlockSpec((None, N_MOD, d), lambda b, s: (b, 0, 0))]
                 + [_const_spec(a.shape, 2) for a in small],
        out_specs=pl.BlockSpec((1, tm, d), lambda b, s: (b, s, 0)),
        out_shape=jax.ShapeDtypeStruct(x.shape, F32),
        input_output_aliases={0: 0},
        compiler_params=pltpu.CompilerParams(dimension_semantics=("parallel", "arbitrary"),
                                             vmem_limit_bytes=VMEM_LIMIT),
        name="dense_ffn",
    )(x, mod_l, *small)


MOE_TB = 512
MOE_SLOTS = 2 * MOE_TB + LANES
SEG_SIZES = tuple(BF16_ROWS << i for i in range(5, -1, -1))


def _segment_copies(fn, src_off, dst_off, nrows):
    pos = jnp.int32(0)
    for size in SEG_SIZES:
        fn(pl.multiple_of(src_off + pos, BF16_ROWS), pl.multiple_of(dst_off + pos, BF16_ROWS), size,
           (nrows & size) != 0)
        pos = pos + (nrows & size)


def _route_kernel(nsb, x_ref, mod_ref, ng_ref, wr_ref, sut_ref, xs_hbm, route_ref, seg_ref,
                  xs_buf, run_ref, sem):
    tb, slots = MOE_TB, MOE_SLOTS
    blk = pl.program_id(0) * nsb + pl.program_id(1)

    @pl.when(blk == 0)
    def _():
        for e in range(N_EXPERTS):
            run_ref[e] = 0

    mod = mod_ref[...]
    h = _mod_norm(x_ref[0], ng_ref[...], mod[3:4], mod[4:5])
    logits = jnp.dot(h, wr_ref[...], precision=HI, preferred_element_type=F32)
    lt = logits.T[0:N_EXPERTS, :]
    eid = lax.broadcasted_iota(I32, (N_EXPERTS, tb), 0)
    m1 = jnp.max(lt, axis=0, keepdims=True)
    i1 = jnp.min(jnp.where(lt == m1, eid, N_EXPERTS), axis=0, keepdims=True)
    lt2 = jnp.where(eid == i1, -jnp.inf, lt)
    m2 = jnp.max(lt2, axis=0, keepdims=True)
    i2 = jnp.min(jnp.where(lt2 == m2, eid, N_EXPERTS), axis=0, keepdims=True)
    e2 = jnp.exp(m2 - m1)
    w_first = 1.0 / (1.0 + e2)
    w_second = e2 / (1.0 + e2)

    sel1, sel2 = eid == i1, eid == i2
    onehot = jnp.where(sel1 | sel2, 1.0, 0.0).astype(F32)
    rank = _dot(onehot.astype(BF16), sut_ref[...])
    counts = [jnp.sum(onehot[e:e + 1, :]).astype(I32) for e in range(N_EXPERTS)]
    cnt16 = [((n + (BF16_ROWS - 1)) >> 4) << 4 for n in counts]
    offs, total = [], jnp.int32(0)
    for e in range(N_EXPERTS):
        offs.append(total)
        total = total + cnt16[e]
    runs = [run_ref[e] for e in range(N_EXPERTS)]

    sub = lax.broadcasted_iota(I32, (N_EXPERTS, 1), 0)

    def expert_column(vals):
        col = jnp.zeros((N_EXPERTS, 1), I32)
        for e in range(N_EXPERTS):
            col = jnp.where(sub == e, vals[e], col)
        return col

    off_col, cnt_col, run_col = expert_column(offs), expert_column(cnt16), expert_column(runs)
    slot_all = off_col.astype(F32) + rank
    slot1 = jnp.sum(jnp.where(sel1, slot_all, 0.0), axis=0, keepdims=True)
    slot2 = jnp.sum(jnp.where(sel2, slot_all, 0.0), axis=0, keepdims=True)

    rowi = lax.broadcasted_iota(I32, (slots, tb), 0)
    perm = jnp.where((rowi == slot1.astype(I32)) | (rowi == slot2.astype(I32)), 1.0, 0.0).astype(BF16)
    xs_buf[...] = _dot(perm, h.astype(BF16)).astype(BF16)

    def start(e):
        def fn(src, dst, size, cond):
            @pl.when(cond)
            def _():
                pltpu.make_async_copy(xs_buf.at[pl.ds(src, size), :], xs_hbm.at[e, pl.ds(dst, size), :], sem).start()
        return fn

    def wait(e):
        def fn(src, dst, size, cond):
            @pl.when(cond)
            def _():
                pltpu.make_async_copy(xs_buf.at[pl.ds(src, size), :], xs_hbm.at[e, pl.ds(dst, size), :], sem).wait()
        return fn

    for e in range(N_EXPERTS):
        _segment_copies(start(e), offs[e], runs[e], cnt16[e])
    for e in range(N_EXPERTS):
        _segment_copies(wait(e), offs[e], runs[e], cnt16[e])

    route_ref[0] = jnp.concatenate([slot1, slot2, w_first, w_second, jnp.zeros((4, tb), F32)], axis=0)
    lane = lax.broadcasted_iota(I32, (N_EXPERTS, LANES), 1)
    seg_ref[0] = jnp.where(lane == 0, off_col, jnp.where(lane == 1, cnt_col, jnp.where(lane == 2, run_col, 0)))
    for e in range(N_EXPERTS):
        run_ref[e] = runs[e] + cnt16[e]


def _route_call(x, mod_l, ng, wr_pad, cap):
    bsz, seq, d = x.shape
    tb = MOE_TB
    nsb = seq // tb
    nblk = bsz * nsb
    ri = lax.broadcasted_iota(I32, (tb, tb), 0)
    ci = lax.broadcasted_iota(I32, (tb, tb), 1)
    sut = (ri < ci).astype(BF16)
    return pl.pallas_call(
        functools.partial(_route_kernel, nsb),
        grid=(bsz, nsb),
        in_specs=[pl.BlockSpec((1, tb, d), lambda b, s: (b, s, 0)),
                  pl.BlockSpec((None, N_MOD, d), lambda b, s: (b, 0, 0)),
                  _const_spec(ng.shape, 2), _const_spec(wr_pad.shape, 2), _const_spec((tb, tb), 2)],
        out_specs=[pl.BlockSpec(memory_space=pl.ANY),
                   pl.BlockSpec((1, 8, tb), lambda b, s: (b * nsb + s, 0, 0)),
                   pl.BlockSpec((1, N_EXPERTS, LANES), lambda b, s: (b * nsb + s, 0, 0))],
        out_shape=[jax.ShapeDtypeStruct((N_EXPERTS, cap, d), BF16),
                   jax.ShapeDtypeStruct((nblk, 8, tb), F32),
                   jax.ShapeDtypeStruct((nblk, N_EXPERTS, LANES), I32)],
        scratch_shapes=[pltpu.VMEM((MOE_SLOTS, d), BF16), pltpu.SMEM((N_EXPERTS,), I32), pltpu.SemaphoreType.DMA],
        compiler_params=pltpu.CompilerParams(dimension_semantics=("arbitrary", "arbitrary"),
                                             vmem_limit_bytes=VMEM_LIMIT),
        name="moe_route",
    )(x, mod_l, ng, wr_pad, sut)


def _gmm_kernel(nf, te_ref, trb_ref, tval_ref, tnv_ref, xs_ref, w1_ref, w3_ref, w2_ref, ys_ref, acc_ref):
    i, j = pl.program_id(0), pl.program_id(1)
    valid = tval_ref[i] == 1

    @pl.when(valid)
    def _():
        tm = xs_ref.shape[0]
        rows = lax.broadcasted_iota(I32, (tm, 1), 0)
        xs = xs_ref[...]
        xs = jnp.where(rows < tnv_ref[i], xs, jnp.zeros_like(xs))
        h1 = _dot(xs, w1_ref[...])
        h3 = _dot(xs, w3_ref[...])
        part = _dot((_silu(h1) * h3).astype(BF16), w2_ref[...])

        @pl.when(j == 0)
        def _():
            acc_ref[...] = part

        @pl.when(j > 0)
        def _():
            acc_ref[...] += part

    @pl.when(valid & (j == nf - 1))
    def _():
        ys_ref[...] = acc_ref[...].astype(BF16)


def _gmm_call(xs, w1, w3, w2, te, trb, tval, tnv, tm, tf):
    ne, cap, d = xs.shape
    dff = w1.shape[-1]
    nf = dff // tf
    nt = te.shape[0]

    def jj(i, j, tval):
        return jnp.where(tval[i] == 1, j, nf - 1)

    return pl.pallas_call(
        functools.partial(_gmm_kernel, nf),
        grid_spec=pltpu.PrefetchScalarGridSpec(
            num_scalar_prefetch=4,
            grid=(nt, nf),
            in_specs=[pl.BlockSpec((None, tm, d), lambda i, j, te, trb, tval, tnv: (te[i], trb[i], 0)),
                      pl.BlockSpec((None, d, tf), lambda i, j, te, trb, tval, tnv: (te[i], 0, jj(i, j, tval))),
                      pl.BlockSpec((None, d, tf), lambda i, j, te, trb, tval, tnv: (te[i], 0, jj(i, j, tval))),
                      pl.BlockSpec((None, tf, d), lambda i, j, te, trb, tval, tnv: (te[i], jj(i, j, tval), 0))],
            out_specs=pl.BlockSpec((None, tm, d), lambda i, j, te, trb, tval, tnv: (te[i], trb[i], 0)),
            scratch_shapes=[pltpu.VMEM((tm, d), F32)]),
        out_shape=jax.ShapeDtypeStruct((ne, cap, d), BF16),
        compiler_params=pltpu.CompilerParams(dimension_semantics=("arbitrary", "arbitrary"),
                                             vmem_limit_bytes=VMEM_LIMIT),
        name="moe_experts",
    )(te, trb, tval, tnv, xs, w1, w3, w2)


def _combine_kernel(final, nsb, seg_ref, x_ref, mod_ref, route_ref, fng_ref, ys_hbm, o_ref, ys_buf, sem):
    tb, slots = MOE_TB, MOE_SLOTS
    blk = pl.program_id(0) * nsb + pl.program_id(1)
    base = blk * (3 * N_EXPERTS)
    offs = [seg_ref[base + e] for e in range(N_EXPERTS)]
    cnt16 = [seg_ref[base + N_EXPERTS + e] for e in range(N_EXPERTS)]
    runs = [seg_ref[base + 2 * N_EXPERTS + e] for e in range(N_EXPERTS)]

    def start(e):
        def fn(dst, src, size, cond):
            @pl.when(cond)
            def _():
                pltpu.make_async_copy(ys_hbm.at[e, pl.ds(src, size), :], ys_buf.at[pl.ds(dst, size), :], sem).start()
        return fn

    def wait(e):
        def fn(dst, src, size, cond):
            @pl.when(cond)
            def _():
                pltpu.make_async_copy(ys_hbm.at[e, pl.ds(src, size), :], ys_buf.at[pl.ds(dst, size), :], sem).wait()
        return fn

    for e in range(N_EXPERTS):
        _segment_copies(start(e), offs[e], runs[e], cnt16[e])
    for e in range(N_EXPERTS):
        _segment_copies(wait(e), offs[e], runs[e], cnt16[e])

    total = offs[N_EXPERTS - 1] + cnt16[N_EXPERTS - 1]
    rowi = lax.broadcasted_iota(I32, (slots, 1), 0)
    ys = ys_buf[...]
    ys = jnp.where(rowi < total, ys, jnp.zeros_like(ys))

    route = jnp.concatenate([route_ref[0], jnp.zeros((LANES - 8, tb), F32)], axis=0).T
    slot1, slot2 = route[:, 0:1].astype(I32), route[:, 1:2].astype(I32)
    lane = lax.broadcasted_iota(I32, (tb, slots), 1)
    comb = (jnp.where(lane == slot1, route[:, 2:3], 0.0) + jnp.where(lane == slot2, route[:, 3:4], 0.0)).astype(BF16)
    out = x_ref[0] + mod_ref[...][5:6] * _dot(comb, ys)
    if final:
        ms = jnp.mean(out * out, axis=-1, keepdims=True)
        out = (out * lax.rsqrt(ms + EPS)) * fng_ref[...]
    o_ref[0] = out


def _combine_call(final, x, mod_l, route, seg_flat, ys, fng):
    bsz, seq, d = x.shape
    tb = MOE_TB
    nsb = seq // tb
    return pl.pallas_call(
        functools.partial(_combine_kernel, final, nsb),
        grid_spec=pltpu.PrefetchScalarGridSpec(
            num_scalar_prefetch=1,
            grid=(bsz, nsb),
            in_specs=[pl.BlockSpec((1, tb, d), lambda b, s, seg: (b, s, 0)),
                      pl.BlockSpec((None, N_MOD, d), lambda b, s, seg: (b, 0, 0)),
                      pl.BlockSpec((1, 8, tb), lambda b, s, seg: (b * nsb + s, 0, 0)),
                      pl.BlockSpec(fng.shape, lambda b, s, seg: (0, 0)),
                      pl.BlockSpec(memory_space=pl.ANY)],
            out_specs=pl.BlockSpec((1, tb, d), lambda b, s, seg: (b, s, 0)),
            scratch_shapes=[pltpu.VMEM((MOE_SLOTS, d), BF16), pltpu.SemaphoreType.DMA]),
        out_shape=jax.ShapeDtypeStruct(x.shape, F32),
        input_output_aliases={1: 0},
        compiler_params=pltpu.CompilerParams(dimension_semantics=("arbitrary", "arbitrary"),
                                             vmem_limit_bytes=VMEM_LIMIT),
        name="moe_combine",
    )(seg_flat, x, mod_l, route, fng, ys)


def _moe_tiles(seg, tm, nt):
    total = seg[-1, :, 2] + seg[-1, :, 1]
    tiles = (total + tm - 1) // tm
    ends = jnp.cumsum(tiles)
    starts = ends - tiles
    ntiles = ends[-1]
    i = jnp.arange(nt, dtype=I32)
    valid = i < ntiles
    ie = jnp.minimum(i, ntiles - 1)
    te = jnp.minimum(jnp.searchsorted(ends, ie, side="right").astype(I32), N_EXPERTS - 1)
    trb = ie - starts[te]
    tnv = jnp.clip(total[te] - trb * tm, 0, tm)
    return te.astype(I32), trb.astype(I32), valid.astype(I32), tnv.astype(I32)


def _moe_layer(final, x, mod_l, ng, wr_pad, w1, w3, w2, fng, tm=1024, tf=512):
    bsz, seq, d = x.shape
    tokens = bsz * seq
    nblk = tokens // MOE_TB
    cap = -(-(tokens + (BF16_ROWS - 1) * nblk) // tm) * tm
    nt = (2 * tokens + (BF16_ROWS - 1) * N_EXPERTS * nblk) // tm + N_EXPERTS
    xs, route, seg = _route_call(x, mod_l, ng, wr_pad, cap)
    te, trb, tval, tnv = _moe_tiles(seg, tm, nt)
    ys = _gmm_call(xs, w1, w3, w2, te, trb, tval, tnv, tm, tf)
    seg_flat = jnp.transpose(seg[:, :, 0:3], (0, 2, 1)).reshape(-1)
    return _combine_call(final, x, mod_l, route, seg_flat, ys, fng)


def kernel(x, c, norm_mix_g, norm_ffn_g, final_norm_g, w_ada, b_ada, w_in, w_out, conv_w, sgu_norm_g, sgu_w, sgu_b,
           gla_w_gate, gla_b_gate, gla_norm_g, hgrn_lower_bounds, hgrn_norm_g, ffn_w1, ffn_w3, ffn_w2, moe_router,
           moe_w1, moe_w3, moe_w2):
    depth = w_in.shape[0]
    bsz, seq, d = x.shape
    assert d == D_MODEL and seq % MOE_TB == 0 and hgrn_lower_bounds.shape[0] == depth

    mod = _ada_call(c, w_ada, b_ada).reshape(depth, bsz, N_MOD, d)

    win = jnp.concatenate([w_in[:, :, :IN_PROJ_SPLIT],
                           jnp.zeros((depth, d, LANES - GLA_GATE_RANK), w_in.dtype),
                           w_in[:, :, IN_PROJ_SPLIT:]], axis=-1).astype(BF16)
    wout = w_out.astype(BF16)
    wgate = jnp.concatenate([gla_w_gate, jnp.zeros((depth, LANES - GLA_GATE_RANK, gla_w_gate.shape[-1]), F32)],
                            axis=1).astype(BF16)
    fng = final_norm_g.reshape(1, d)

    for layer in range(depth):
        x = _mixer_call(
            layer, x, mod[layer], norm_mix_g[layer].reshape(1, d), win[layer], wout[layer], conv_w[layer],
            sgu_norm_g[layer].reshape(1, GROUP_WIDTH), sgu_w[layer].reshape(HEADS * SUPER, SUPER),
            jnp.repeat(sgu_b[layer].T, HEAD_DIM, axis=1), wgate[layer], gla_b_gate[layer].reshape(1, -1),
            jnp.tile(gla_norm_g[layer], HEADS).reshape(1, GROUP_WIDTH), hgrn_lower_bounds,
            jnp.tile(hgrn_norm_g[layer], HEADS).reshape(1, GROUP_WIDTH))
        final = layer == depth - 1
        idx = layer // 2
        ng = norm_ffn_g[layer].reshape(1, d)
        if layer % 2 == 0:
            x = _ffn_call(final, x, mod[layer], ng, ffn_w1[idx].astype(BF16), ffn_w3[idx].astype(BF16),
                          ffn_w2[idx].astype(BF16), fng)
        else:
            wr_pad = jnp.concatenate([moe_router[idx], jnp.zeros((d, LANES - N_EXPERTS), F32)], axis=1)
            x = _moe_layer(final, x, mod[layer], ng, wr_pad, moe_w1[idx].astype(BF16), moe_w3[idx].astype(BF16),
                           moe_w2[idx].astype(BF16), fng)
    return x
```

```python
import functools

import jax
import jax.numpy as jnp
from jax import lax
from jax.experimental import pallas as pl
from jax.experimental.pallas import tpu as pltpu

F32 = jnp.float32
BF16 = jnp.bfloat16
I32 = jnp.int32
EPS = 1e-6

D_MODEL = 1024
N_MOD = 6
GROUP_WIDTH = 256
HEADS = 4
HEAD_DIM = 64
GLA_KEY_DIM = 32
GLA_GATE_RANK = 16
GLA_GATE_NORMALIZER = 16.0
LA_CHUNK_LOG2 = 4
SUPER = 128
SUPER_LOG2 = 7
N_EXPERTS = 8
LANES = 128
BF16_ROWS = 16
BF16_ROWS_LOG2 = 4

C_CB, C_CC, C_CX, C_SU, C_SV = 0, 256, 512, 768, 1024
C_AQ, C_AK, C_AV, C_AG, C_AR = 1280, 1408, 1536, 1792, 1920
C_HQ, C_HF, C_HI, C_HG = 2176, 2432, 2688, 2944
IN_PROJ_PAD = 3200
IN_PROJ_SPLIT = 1808

VMEM_LIMIT = 56 * 1024 * 1024

MIXER_ROWS = 256
FFN_ROWS = 512
MOE_TB = 512
MOE_SLOTS = 2 * MOE_TB + LANES
MOE_TILE_ROWS = 512
MOE_FF_CHUNK = 512


def _silu(x):
    return x * jax.nn.sigmoid(x)


def _dot(a, b):
    return jnp.dot(a, b, preferred_element_type=F32)


def _dot_nt(a, b):
    return lax.dot_general(a, b, (((1,), (1,)), ((), ())), preferred_element_type=F32)


def _dot_tn(a, b):
    return lax.dot_general(a, b, (((0,), (0,)), ((), ())), preferred_element_type=F32)


def _split_bf16(x, pieces):
    out, r = [], x
    for _ in range(pieces):
        t = r.astype(BF16)
        out.append(t)
        r = r - t.astype(F32)
    return out


def _mod_norm(xt, g, shift, scale):
    ms = jnp.mean(xt * xt, axis=-1, keepdims=True)
    return (xt * lax.rsqrt(ms + EPS)) * g * (1.0 + scale) + shift


def _const_spec(shape):
    zeros = (0,) * len(shape)
    return pl.BlockSpec(shape, lambda *_: zeros, pipeline_mode=pl.Buffered(1))


def _layer_spec(arr, layer):
    zeros = (0,) * (arr.ndim - 1)
    return pl.BlockSpec((None,) + arr.shape[1:], lambda *_: (layer,) + zeros, pipeline_mode=pl.Buffered(1))


def _mod_spec(layer, d):
    return pl.BlockSpec((None, None, N_MOD, d), lambda b, s, *_: (layer, b, 0, 0))


def _ada_kernel(c_ref, w_ref, b_ref, o_ref):
    cond = _silu(c_ref[...])
    o_ref[0] = _dot(cond.astype(BF16), w_ref[0].astype(BF16)) + b_ref[0]


def _ada_call(c, w_ada, b_ada):
    depth, d, n = w_ada.shape
    bsz = c.shape[0]
    tn = n // 4
    return pl.pallas_call(
        _ada_kernel,
        grid=(depth, n // tn),
        in_specs=[pl.BlockSpec((bsz, d), lambda l, j: (0, 0)),
                  pl.BlockSpec((1, d, tn), lambda l, j: (l, 0, j)),
                  pl.BlockSpec((1, 1, tn), lambda l, j: (l, 0, j))],
        out_specs=pl.BlockSpec((1, bsz, tn), lambda l, j: (l, 0, j)),
        out_shape=jax.ShapeDtypeStruct((depth, bsz, n), F32),
        compiler_params=pltpu.CompilerParams(dimension_semantics=("arbitrary", "arbitrary"),
                                             vmem_limit_bytes=VMEM_LIMIT),
        name="adaln_mod",
    )(c, w_ada, b_ada.reshape(depth, 1, n))


def _pair_masks():
    t = lax.broadcasted_iota(I32, (SUPER, HEADS * SUPER), 0)
    s = lax.broadcasted_iota(I32, (SUPER, HEADS * SUPER), 1) & (SUPER - 1)
    masks = [((t >> LA_CHUNK_LOG2) == (s >> LA_CHUNK_LOG2)) & (s <= t)]
    for sh in range(LA_CHUNK_LOG2 + 1, SUPER_LOG2 + 1):
        half = 1 << (sh - 1)
        masks.append(((t >> sh) == (s >> sh)) & ((t & half) != 0) & ((s & half) == 0))
    return masks


def _level_refs(c):
    kt = c.shape[-1]
    n16 = SUPER >> LA_CHUNK_LOG2
    mid = (1 << LA_CHUNK_LOG2) // 2
    refs = [jnp.broadcast_to(c.reshape(n16, 1 << LA_CHUNK_LOG2, kt)[:, mid:mid + 1, :],
                             (n16, 1 << LA_CHUNK_LOG2, kt)).reshape(SUPER, kt)]
    for sh in range(LA_CHUNK_LOG2 + 1, SUPER_LOG2 + 1):
        nb, bs, half = SUPER >> sh, 1 << sh, 1 << (sh - 1)
        refs.append(jnp.broadcast_to(c.reshape(nb, bs, kt)[:, half - 1:half, :], (nb, bs, kt)).reshape(SUPER, kt))
    return refs


def _head_rows(a):
    w = a.shape[-1] // HEADS
    lane_head = lax.broadcasted_iota(I32, a.shape, 1) // w
    return jnp.concatenate([jnp.where(lane_head == h, a, 0.0).astype(BF16) for h in range(HEADS)], axis=0)


def _gated_linear_attention_step(q, k, v, log_a, st_ref, ltri, masks):
    kt = q.shape[-1]
    hk = kt // HEADS
    c = None
    for piece in _split_bf16(log_a, 3):
        term = _dot(ltri, piece)
        c = term if c is None else c + term

    scores = None
    for lvl, (cref, mask) in enumerate(zip(_level_refs(c), masks)):
        dq, dk = c - cref, cref - c
        if lvl > 0:
            dq, dk = jnp.minimum(dq, 0.0), jnp.minimum(dk, 0.0)
        qe = (q * jnp.exp(dq)).astype(BF16)
        s_lvl = _dot_nt(qe, _head_rows(k * jnp.exp(dk)))
        scores = jnp.where(mask, s_lvl, 0.0 if scores is None else scores)
    o = _dot(scores.astype(BF16), _head_rows(v))

    st = st_ref[...]
    o = o + _dot_nt((q * jnp.exp(c)).astype(BF16), st.astype(BF16))
    c_last = c[SUPER - 1:SUPER, :]
    kd = (k * jnp.exp(c_last - c)).astype(BF16)
    upd = _dot_tn(v.astype(BF16), kd)
    diag = (lax.broadcasted_iota(I32, (GROUP_WIDTH, kt), 0) // HEAD_DIM) == (
        lax.broadcasted_iota(I32, (GROUP_WIDTH, kt), 1) // hk)
    st_ref[...] = st * jnp.exp(c_last) + jnp.where(diag, upd, 0.0)
    return o


def _head_rms_gate(o, head_mean, g, gate):
    ms = _dot((o * o).astype(BF16), head_mean)
    return (o * lax.rsqrt(ms + EPS)) * g * _silu(gate)


def _mixer_kernel(layer, ts, x_ref, mod_ref, ng_ref, win_ref, wout_ref, convw_ref, sgug_ref, sguw_ref, sgub_ref,
                  wgate_ref, bgate_ref, glag_ref, hlb_ref, hgng_ref, o_ref,
                  p_ref, y_ref, zbuf, st_gla, st_hgrn):
    @pl.when(pl.program_id(1) == 0)
    def _():
        zbuf[0:8, :] = jnp.zeros((8, GROUP_WIDTH), F32)
        st_gla[...] = jnp.zeros_like(st_gla)
        st_hgrn[...] = jnp.zeros_like(st_hgrn)

    xt = x_ref[0]
    mod = mod_ref[...]
    h = _mod_norm(xt, ng_ref[...], mod[0:1], mod[1:2])
    p_ref[...] = _dot(h.astype(BF16), win_ref[...])

    z = p_ref[:, C_CC:C_CC + GROUP_WIDTH] * p_ref[:, C_CX:C_CX + GROUP_WIDTH]
    zbuf[8:8 + ts, :] = z
    cw = convw_ref[...]
    conv = cw[0:1] * zbuf[6:6 + ts, :] + cw[1:2] * zbuf[7:7 + ts, :] + cw[2:3] * z
    y_ref[:, 0:GROUP_WIDTH] = (p_ref[:, C_CB:C_CB + GROUP_WIDTH] * conv).astype(BF16)
    zbuf[0:8, :] = zbuf[ts:ts + 8, :]

    masks = _pair_masks()
    ri = lax.broadcasted_iota(I32, (SUPER, SUPER), 0)
    ci = lax.broadcasted_iota(I32, (SUPER, SUPER), 1)
    ltri = jnp.where(ci <= ri, 1.0, 0.0).astype(BF16)
    t_ws = lax.broadcasted_iota(I32, (SUPER, HEADS * SUPER), 0)
    s_ws = lax.broadcasted_iota(I32, (SUPER, HEADS * SUPER), 1) & (SUPER - 1)
    ws = jnp.where(s_ws <= t_ws, sguw_ref[...], 0.0).astype(BF16)
    hr = lax.broadcasted_iota(I32, (GROUP_WIDTH, GROUP_WIDTH), 0) // HEAD_DIM
    hc = lax.broadcasted_iota(I32, (GROUP_WIDTH, GROUP_WIDTH), 1) // HEAD_DIM
    head_mean = jnp.where(hr == hc, 1.0 / HEAD_DIM, 0.0).astype(BF16)
    lb_all = hlb_ref[...]
    lb_e = jnp.exp(lb_all - jnp.max(lb_all, axis=0, keepdims=True))
    lb_sm = lb_e / jnp.sum(lb_e, axis=0, keepdims=True)
    lb = jnp.zeros((1, GROUP_WIDTH), F32)
    for j in range(1, layer + 1):
        lb = lb + lb_sm[j:j + 1]

    for i in range(ts // SUPER):
        r = slice(i * SUPER, (i + 1) * SUPER)
        sv = p_ref[r, C_SV:C_SV + GROUP_WIDTH]
        mu = jnp.mean(sv, axis=-1, keepdims=True)
        dv = sv - mu
        var = jnp.mean(dv * dv, axis=-1, keepdims=True)
        vn = (dv * lax.rsqrt(var + EPS)) * sgug_ref[...]
        mixed = _dot(ws, _head_rows(vn)) + sgub_ref[...]
        y_ref[r, GROUP_WIDTH:2 * GROUP_WIDTH] = (p_ref[r, C_SU:C_SU + GROUP_WIDTH] * mixed).astype(BF16)

        gate_logit = _dot(p_ref[r, C_AG:C_AG + LANES].astype(BF16), wgate_ref[...]) + bgate_ref[...]
        log_a = (jnp.minimum(gate_logit, 0.0) - jnp.log1p(jnp.exp(-jnp.abs(gate_logit)))) * (1.0 / GLA_GATE_NORMALIZER)
        o = _gated_linear_attention_step(
            p_ref[r, C_AQ:C_AQ + LANES] * (GLA_KEY_DIM ** -0.5), p_ref[r, C_AK:C_AK + LANES],
            p_ref[r, C_AV:C_AV + GROUP_WIDTH], log_a, st_gla, ltri, masks)
        y_ref[r, 2 * GROUP_WIDTH:3 * GROUP_WIDTH] = _head_rms_gate(
            o, head_mean, glag_ref[...], p_ref[r, C_AR:C_AR + GROUP_WIDTH]).astype(BF16)

        f = lb + (1.0 - lb) * jax.nn.sigmoid(p_ref[r, C_HF:C_HF + GROUP_WIDTH])
        o = _gated_linear_attention_step(
            p_ref[r, C_HQ:C_HQ + GROUP_WIDTH], 1.0 - f, p_ref[r, C_HI:C_HI + GROUP_WIDTH], jnp.log(f),
            st_hgrn, ltri, masks)
        y_ref[r, 3 * GROUP_WIDTH:4 * GROUP_WIDTH] = _head_rms_gate(
            o, head_mean, hgng_ref[...], p_ref[r, C_HG:C_HG + GROUP_WIDTH]).astype(BF16)

    o_ref[0] = xt + mod[2:3] * _dot(y_ref[...], wout_ref[...])


def _mixer_call(layer, x, mod, ng, win, wout, convw, sgug, sguw, sgub, wgate, bgate, glag, hlb, hgng):
    bsz, seq, d = x.shape
    ts = MIXER_ROWS
    small = [convw, sgug, sguw, sgub, wgate, bgate, glag, hlb, hgng]
    return pl.pallas_call(
        functools.partial(_mixer_kernel, layer, ts),
        grid=(bsz, seq // ts),
        in_specs=[pl.BlockSpec((1, ts, d), lambda b, s: (b, s, 0)), _mod_spec(layer, d), _const_spec(ng.shape),
                  _layer_spec(win, layer), _layer_spec(wout, layer)] + [_const_spec(a.shape) for a in small],
        out_specs=pl.BlockSpec((1, ts, d), lambda b, s: (b, s, 0)),
        out_shape=jax.ShapeDtypeStruct(x.shape, F32),
        scratch_shapes=[pltpu.VMEM((ts, IN_PROJ_PAD), F32),
                        pltpu.VMEM((ts, d), BF16),
                        pltpu.VMEM((ts + 8, GROUP_WIDTH), F32),
                        pltpu.VMEM((GROUP_WIDTH, HEADS * GLA_KEY_DIM), F32),
                        pltpu.VMEM((GROUP_WIDTH, GROUP_WIDTH), F32)],
        input_output_aliases={0: 0} if layer > 0 else {},
        compiler_params=pltpu.CompilerParams(dimension_semantics=("parallel", "arbitrary"),
                                             vmem_limit_bytes=VMEM_LIMIT),
        name=f"mixer_l{layer}",
    )(x, mod, ng, win, wout, *small)


def _final_norm(out, g):
    ms = jnp.mean(out * out, axis=-1, keepdims=True)
    return (out * lax.rsqrt(ms + EPS)) * g


def _ffn_kernel(final, x_ref, mod_ref, ng_ref, w1_ref, w3_ref, w2_ref, fng_ref, o_ref):
    xt = x_ref[0]
    mod = mod_ref[...]
    hb = _mod_norm(xt, ng_ref[...], mod[3:4], mod[4:5]).astype(BF16)
    h1 = _dot(hb, w1_ref[...])
    h3 = _dot(hb, w3_ref[...])
    a = (_silu(h1) * h3).astype(BF16)
    out = xt + mod[5:6] * _dot(a, w2_ref[...])
    o_ref[0] = _final_norm(out, fng_ref[...]) if final else out


def _ffn_call(final, layer, x, mod, ng, w1, w3, w2, fng):
    bsz, seq, d = x.shape
    tm = FFN_ROWS
    idx = layer // 2
    return pl.pallas_call(
        functools.partial(_ffn_kernel, final),
        grid=(bsz, seq // tm),
        in_specs=[pl.BlockSpec((1, tm, d), lambda b, s: (b, s, 0)), _mod_spec(layer, d), _const_spec(ng.shape),
                  _layer_spec(w1, idx), _layer_spec(w3, idx), _layer_spec(w2, idx), _const_spec(fng.shape)],
        out_specs=pl.BlockSpec((1, tm, d), lambda b, s: (b, s, 0)),
        out_shape=jax.ShapeDtypeStruct(x.shape, F32),
        input_output_aliases={0: 0},
        compiler_params=pltpu.CompilerParams(dimension_semantics=("parallel", "arbitrary"),
                                             vmem_limit_bytes=VMEM_LIMIT),
        name="dense_ffn",
    )(x, mod, ng, w1, w3, w2, fng)


SEG_SIZES = tuple(BF16_ROWS << i for i in range(5, -1, -1))


def _segment_copies(fn, buf_off, hbm_off, nrows):
    pos = jnp.int32(0)
    for size in SEG_SIZES:
        fn(pl.multiple_of(buf_off + pos, BF16_ROWS), pl.multiple_of(hbm_off + pos, BF16_ROWS), size,
           (nrows & size) != 0)
        pos = pos + (nrows & size)


def _route_kernel(nsb, x_ref, mod_ref, ng_ref, wr_ref, sut_ref, xs_hbm, route_ref, seg_ref,
                  xs_buf, run_ref, sem):
    tb, slots = MOE_TB, MOE_SLOTS
    blk = pl.program_id(0) * nsb + pl.program_id(1)

    @pl.when(blk == 0)
    def _():
        for e in range(N_EXPERTS):
            run_ref[e] = 0

    mod = mod_ref[...]
    h = _mod_norm(x_ref[0], ng_ref[...], mod[3:4], mod[4:5])
    h_hi, h_lo = _split_bf16(h, 2)
    lg = _dot(h_hi, wr_ref[...]) + _dot(h_lo, wr_ref[...])
    logits = lg[:, 0:LANES] + lg[:, LANES:2 * LANES]
    lt = logits.T[0:N_EXPERTS, :]
    eid = lax.broadcasted_iota(I32, (N_EXPERTS, tb), 0)
    m1 = jnp.max(lt, axis=0, keepdims=True)
    i1 = jnp.min(jnp.where(lt == m1, eid, N_EXPERTS), axis=0, keepdims=True)
    lt2 = jnp.where(eid == i1, -jnp.inf, lt)
    m2 = jnp.max(lt2, axis=0, keepdims=True)
    i2 = jnp.min(jnp.where(lt2 == m2, eid, N_EXPERTS), axis=0, keepdims=True)
    e2 = jnp.exp(m2 - m1)
    w_first = 1.0 / (1.0 + e2)
    w_second = e2 / (1.0 + e2)

    sel1, sel2 = eid == i1, eid == i2
    onehot = jnp.where(sel1 | sel2, 1.0, 0.0).astype(F32)
    rank = _dot(onehot.astype(BF16), sut_ref[...])
    counts = [jnp.sum(onehot[e:e + 1, :]).astype(I32) for e in range(N_EXPERTS)]
    cnt16 = [((n + (BF16_ROWS - 1)) >> BF16_ROWS_LOG2) << BF16_ROWS_LOG2 for n in counts]
    offs, total = [], jnp.int32(0)
    for e in range(N_EXPERTS):
        offs.append(total)
        total = total + cnt16[e]
    runs = [run_ref[e] for e in range(N_EXPERTS)]

    sub = lax.broadcasted_iota(I32, (N_EXPERTS, 1), 0)

    def expert_column(vals):
        col = jnp.zeros((N_EXPERTS, 1), I32)
        for e in range(N_EXPERTS):
            col = jnp.where(sub == e, vals[e], col)
        return col

    off_col, cnt_col, run_col = expert_column(offs), expert_column(cnt16), expert_column(runs)
    slot_all = off_col.astype(F32) + rank
    slot1 = jnp.sum(jnp.where(sel1, slot_all, 0.0), axis=0, keepdims=True)
    slot2 = jnp.sum(jnp.where(sel2, slot_all, 0.0), axis=0, keepdims=True)

    rowi = lax.broadcasted_iota(I32, (slots, tb), 0)
    perm = jnp.where((rowi == slot1.astype(I32)) | (rowi == slot2.astype(I32)), 1.0, 0.0).astype(BF16)
    xs_buf[...] = _dot(perm, h_hi).astype(BF16)

    def copy(e, buf_row, hbm_row, size):
        return pltpu.make_async_copy(xs_buf.at[pl.ds(buf_row, size), :], xs_hbm.at[e, pl.ds(hbm_row, size), :], sem)

    def start(e):
        def fn(buf_row, hbm_row, size, cond):
            @pl.when(cond)
            def _():
                copy(e, buf_row, hbm_row, size).start()
        return fn

    def wait(e):
        def fn(buf_row, hbm_row, size, cond):
            @pl.when(cond)
            def _():
                copy(e, buf_row, hbm_row, size).wait()
        return fn

    for e in range(N_EXPERTS):
        _segment_copies(start(e), offs[e], runs[e], cnt16[e])
    for e in range(N_EXPERTS):
        _segment_copies(wait(e), offs[e], runs[e], cnt16[e])

    route_ref[0] = jnp.concatenate([slot1, slot2, w_first, w_second, jnp.zeros((4, tb), F32)], axis=0)
    lane = lax.broadcasted_iota(I32, (N_EXPERTS, LANES), 1)
    seg_ref[0] = jnp.where(lane == 0, off_col, jnp.where(lane == 1, cnt_col, jnp.where(lane == 2, run_col, 0)))
    for e in range(N_EXPERTS):
        run_ref[e] = runs[e] + cnt16[e]


def _route_call(layer, x, mod, ng, wr2, cap):
    bsz, seq, d = x.shape
    tb = MOE_TB
    nsb = seq // tb
    nblk = bsz * nsb
    ri = lax.broadcasted_iota(I32, (tb, tb), 0)
    ci = lax.broadcasted_iota(I32, (tb, tb), 1)
    sut = (ri < ci).astype(BF16)
    return pl.pallas_call(
        functools.partial(_route_kernel, nsb),
        grid=(bsz, nsb),
        in_specs=[pl.BlockSpec((1, tb, d), lambda b, s: (b, s, 0)), _mod_spec(layer, d),
                  _const_spec(ng.shape), _const_spec(wr2.shape), _const_spec((tb, tb))],
        out_specs=[pl.BlockSpec(memory_space=pl.ANY),
                   pl.BlockSpec((1, 8, tb), lambda b, s: (b * nsb + s, 0, 0)),
                   pl.BlockSpec((1, N_EXPERTS, LANES), lambda b, s: (b * nsb + s, 0, 0))],
        out_shape=[jax.ShapeDtypeStruct((N_EXPERTS, cap, d), BF16),
                   jax.ShapeDtypeStruct((nblk, 8, tb), F32),
                   jax.ShapeDtypeStruct((nblk, N_EXPERTS, LANES), I32)],
        scratch_shapes=[pltpu.VMEM((MOE_SLOTS, d), BF16), pltpu.SMEM((N_EXPERTS,), I32), pltpu.SemaphoreType.DMA],
        compiler_params=pltpu.CompilerParams(dimension_semantics=("arbitrary", "arbitrary"),
                                             vmem_limit_bytes=VMEM_LIMIT),
        name="moe_route",
    )(x, mod, ng, wr2, sut)


def _gmm_kernel(te_ref, trb_ref, tval_ref, tnv_ref, xs_ref, w1_ref, w3_ref, w2_ref, ys_ref):
    i = pl.program_id(0)

    @pl.when(tval_ref[i] == 1)
    def _():
        tm = xs_ref.shape[0]
        rows = lax.broadcasted_iota(I32, (tm, 1), 0)
        xs = xs_ref[...]
        xs = jnp.where(rows < tnv_ref[i], xs, jnp.zeros_like(xs))
        acc = None
        for j in range(w1_ref.shape[-1] // MOE_FF_CHUNK):
            cols = slice(j * MOE_FF_CHUNK, (j + 1) * MOE_FF_CHUNK)
            h1 = _dot(xs, w1_ref[:, cols])
            h3 = _dot(xs, w3_ref[:, cols])
            part = _dot((_silu(h1) * h3).astype(BF16), w2_ref[cols, :])
            acc = part if acc is None else acc + part
        ys_ref[...] = acc.astype(BF16)


def _gmm_call(idx, xs, w1, w3, w2, te, trb, tval, tnv):
    ne, cap, d = xs.shape
    tm = MOE_TILE_ROWS
    nt = te.shape[0]

    def wspec(w):
        return pl.BlockSpec((None, None) + w.shape[2:], lambda i, te, trb, tval, tnv: (idx, te[i], 0, 0),
                            pipeline_mode=pl.Buffered(1))

    return pl.pallas_call(
        _gmm_kernel,
        grid_spec=pltpu.PrefetchScalarGridSpec(
            num_scalar_prefetch=4,
            grid=(nt,),
            in_specs=[pl.BlockSpec((None, tm, d), lambda i, te, trb, tval, tnv: (te[i], trb[i], 0)),
                      wspec(w1), wspec(w3), wspec(w2)],
            out_specs=pl.BlockSpec((None, tm, d), lambda i, te, trb, tval, tnv: (te[i], trb[i], 0))),
        out_shape=jax.ShapeDtypeStruct((ne, cap, d), BF16),
        compiler_params=pltpu.CompilerParams(dimension_semantics=("arbitrary",), vmem_limit_bytes=VMEM_LIMIT),
        name="moe_experts",
    )(te, trb, tval, tnv, xs, w1, w3, w2)


def _combine_kernel(final, nsb, seg_ref, x_ref, mod_ref, route_ref, fng_ref, ys_hbm, o_ref, ys_buf, sem):
    tb, slots = MOE_TB, MOE_SLOTS
    blk = pl.program_id(0) * nsb + pl.program_id(1)
    base = blk * (3 * N_EXPERTS)
    offs = [seg_ref[base + e] for e in range(N_EXPERTS)]
    cnt16 = [seg_ref[base + N_EXPERTS + e] for e in range(N_EXPERTS)]
    runs = [seg_ref[base + 2 * N_EXPERTS + e] for e in range(N_EXPERTS)]

    def copy(e, buf_row, hbm_row, size):
        return pltpu.make_async_copy(ys_hbm.at[e, pl.ds(hbm_row, size), :], ys_buf.at[pl.ds(buf_row, size), :], sem)

    def start(e):
        def fn(buf_row, hbm_row, size, cond):
            @pl.when(cond)
            def _():
                copy(e, buf_row, hbm_row, size).start()
        return fn

    def wait(e):
        def fn(buf_row, hbm_row, size, cond):
            @pl.when(cond)
            def _():
                copy(e, buf_row, hbm_row, size).wait()
        return fn

    for e in range(N_EXPERTS):
        _segment_copies(start(e), offs[e], runs[e], cnt16[e])
    for e in range(N_EXPERTS):
        _segment_copies(wait(e), offs[e], runs[e], cnt16[e])

    total = offs[N_EXPERTS - 1] + cnt16[N_EXPERTS - 1]
    rowi = lax.broadcasted_iota(I32, (slots, 1), 0)
    ys = ys_buf[...]
    ys = jnp.where(rowi < total, ys, jnp.zeros_like(ys))

    route = jnp.concatenate([route_ref[0], jnp.zeros((LANES - 8, tb), F32)], axis=0).T
    slot1, slot2 = route[:, 0:1].astype(I32), route[:, 1:2].astype(I32)
    lane = lax.broadcasted_iota(I32, (tb, slots), 1)
    comb = (jnp.where(lane == slot1, route[:, 2:3], 0.0) + jnp.where(lane == slot2, route[:, 3:4], 0.0)).astype(BF16)
    out = x_ref[0] + mod_ref[...][5:6] * _dot(comb, ys)
    o_ref[0] = _final_norm(out, fng_ref[...]) if final else out


def _combine_call(final, layer, x, mod, route, seg_flat, ys, fng):
    bsz, seq, d = x.shape
    tb = MOE_TB
    nsb = seq // tb
    return pl.pallas_call(
        functools.partial(_combine_kernel, final, nsb),
        grid_spec=pltpu.PrefetchScalarGridSpec(
            num_scalar_prefetch=1,
            grid=(bsz, nsb),
            in_specs=[pl.BlockSpec((1, tb, d), lambda b, s, seg: (b, s, 0)), _mod_spec(layer, d),
                      pl.BlockSpec((1, 8, tb), lambda b, s, seg: (b * nsb + s, 0, 0)),
                      _const_spec(fng.shape),
                      pl.BlockSpec(memory_space=pl.ANY)],
            out_specs=pl.BlockSpec((1, tb, d), lambda b, s, seg: (b, s, 0)),
            scratch_shapes=[pltpu.VMEM((MOE_SLOTS, d), BF16), pltpu.SemaphoreType.DMA]),
        out_shape=jax.ShapeDtypeStruct(x.shape, F32),
        input_output_aliases={1: 0},
        compiler_params=pltpu.CompilerParams(dimension_semantics=("arbitrary", "arbitrary"),
                                             vmem_limit_bytes=VMEM_LIMIT),
        name="moe_combine",
    )(seg_flat, x, mod, route, fng, ys)


def _moe_tiles(seg, tm, nt):
    total = seg[-1, :, 2] + seg[-1, :, 1]
    tiles = (total + tm - 1) // tm
    ends = jnp.cumsum(tiles)
    starts = ends - tiles
    ntiles = ends[-1]
    i = jnp.arange(nt, dtype=I32)
    valid = i < ntiles
    ie = jnp.minimum(i, ntiles - 1)
    te = jnp.minimum(jnp.sum((ie[:, None] >= ends[None, :]).astype(I32), axis=1), N_EXPERTS - 1)
    trb = ie - starts[te]
    tnv = jnp.clip(total[te] - trb * tm, 0, tm)
    return te.astype(I32), trb.astype(I32), valid.astype(I32), tnv.astype(I32)


def _moe_layer(final, layer, x, mod, ng, wr2, w1, w3, w2, fng):
    bsz, seq, d = x.shape
    tokens = bsz * seq
    nblk = tokens // MOE_TB
    tm = MOE_TILE_ROWS
    cap = -(-(tokens + (BF16_ROWS - 1) * nblk) // tm) * tm
    nt = (2 * tokens + (BF16_ROWS - 1) * N_EXPERTS * nblk) // tm + N_EXPERTS
    xs, route, seg = _route_call(layer, x, mod, ng, wr2, cap)
    te, trb, tval, tnv = _moe_tiles(seg, tm, nt)
    ys = _gmm_call(layer // 2, xs, w1, w3, w2, te, trb, tval, tnv)
    seg_flat = jnp.transpose(seg[:, :, 0:3], (0, 2, 1)).reshape(-1)
    return _combine_call(final, layer, x, mod, route, seg_flat, ys, fng)


def kernel(x, c, norm_mix_g, norm_ffn_g, final_norm_g, w_ada, b_ada, w_in, w_out, conv_w, sgu_norm_g, sgu_w, sgu_b,
           gla_w_gate, gla_b_gate, gla_norm_g, hgrn_lower_bounds, hgrn_norm_g, ffn_w1, ffn_w3, ffn_w2, moe_router,
           moe_w1, moe_w3, moe_w2):
    depth = w_in.shape[0]
    bsz, seq, d = x.shape
    assert d == D_MODEL and seq % MOE_TB == 0 and hgrn_lower_bounds.shape[0] == depth

    mod = _ada_call(c, w_ada, b_ada).reshape(depth, bsz, N_MOD, d)

    win = jnp.concatenate([w_in[:, :, :IN_PROJ_SPLIT],
                           jnp.zeros((depth, d, LANES - GLA_GATE_RANK), w_in.dtype),
                           w_in[:, :, IN_PROJ_SPLIT:]], axis=-1).astype(BF16)
    wout = w_out.astype(BF16)
    wgate = jnp.concatenate([gla_w_gate, jnp.zeros((depth, LANES - GLA_GATE_RANK, gla_w_gate.shape[-1]), F32)],
                            axis=1).astype(BF16)
    fng = final_norm_g.reshape(1, d)
    ffn_w = [w.astype(BF16) for w in (ffn_w1, ffn_w3, ffn_w2)]
    moe_w = [w.astype(BF16) for w in (moe_w1, moe_w3, moe_w2)]
    wr = jnp.concatenate([moe_router, jnp.zeros(moe_router.shape[:2] + (LANES - N_EXPERTS,), F32)], axis=-1)
    wr_hi = wr.astype(BF16)
    wr2 = jnp.concatenate([wr_hi, (wr - wr_hi.astype(F32)).astype(BF16)], axis=-1)

    for layer in range(depth):
        x = _mixer_call(
            layer, x, mod, norm_mix_g[layer].reshape(1, d), win, wout, conv_w[layer],
            sgu_norm_g[layer].reshape(1, GROUP_WIDTH),
            jnp.transpose(sgu_w[layer], (1, 0, 2)).reshape(SUPER, HEADS * SUPER),
            jnp.repeat(sgu_b[layer].T, HEAD_DIM, axis=1), wgate[layer], gla_b_gate[layer].reshape(1, -1),
            jnp.tile(gla_norm_g[layer], HEADS).reshape(1, GROUP_WIDTH), hgrn_lower_bounds,
            jnp.tile(hgrn_norm_g[layer], HEADS).reshape(1, GROUP_WIDTH))
        final = layer == depth - 1
        ng = norm_ffn_g[layer].reshape(1, d)
        if layer % 2 == 0:
            x = _ffn_call(final, layer, x, mod, ng, *ffn_w, fng)
        else:
            x = _moe_layer(final, layer, x, mod, ng, wr2[layer // 2], *moe_w, fng)
    return x
```

```python
import functools

import jax
import jax.numpy as jnp
from jax import lax
from jax.experimental import pallas as pl
from jax.experimental.pallas import tpu as pltpu

F32 = jnp.float32
BF16 = jnp.bfloat16
I32 = jnp.int32
EPS = 1e-6

D_MODEL = 1024
N_MOD = 6
GROUP_WIDTH = 256
HEADS = 4
HEAD_DIM = 64
GLA_KEY_DIM = 32
GLA_GATE_RANK = 16
GLA_GATE_NORMALIZER = 16.0
LA_CHUNK_LOG2 = 4
SUPER = 128
SUPER_LOG2 = 7
N_EXPERTS = 8
LANES = 128
BF16_ROWS = 16
BF16_ROWS_LOG2 = 4

C_CB, C_CC, C_CX, C_SU, C_SV = 0, 256, 512, 768, 1024
C_AQ, C_AK, C_AV, C_AG, C_AR = 1280, 1408, 1536, 1792, 1920
C_HQ, C_HF, C_HI, C_HG = 2176, 2432, 2688, 2944
IN_PROJ_PAD = 3200
IN_PROJ_SPLIT = 1808

VMEM_LIMIT = 56 * 1024 * 1024

MIXER_ROWS = 512
FFN_ROWS = 512
MOE_TB = 512
MOE_SLOTS = 2 * MOE_TB + LANES
MOE_TILE_ROWS = 512
MOE_FF_CHUNK = 512


def _silu(x):
    return x * jax.nn.sigmoid(x)


def _dot(a, b):
    return jnp.dot(a, b, preferred_element_type=F32)


def _dot_nt(a, b):
    return lax.dot_general(a, b, (((1,), (1,)), ((), ())), preferred_element_type=F32)


def _dot_tn(a, b):
    return lax.dot_general(a, b, (((0,), (0,)), ((), ())), preferred_element_type=F32)


def _split_bf16(x, pieces):
    out, r = [], x
    for _ in range(pieces):
        t = r.astype(BF16)
        out.append(t)
        r = r - t.astype(F32)
    return out


def _mod_norm(xt, g, shift, scale):
    ms = jnp.mean(xt * xt, axis=-1, keepdims=True)
    return (xt * lax.rsqrt(ms + EPS)) * g * (1.0 + scale) + shift


def _const_spec(shape):
    zeros = (0,) * len(shape)
    return pl.BlockSpec(shape, lambda *_: zeros, pipeline_mode=pl.Buffered(1))


def _layer_spec(arr, layer):
    zeros = (0,) * (arr.ndim - 1)
    return pl.BlockSpec((None,) + arr.shape[1:], lambda *_: (layer,) + zeros, pipeline_mode=pl.Buffered(1))


def _mod_spec(layer, d):
    return pl.BlockSpec((None, None, N_MOD, d), lambda b, s, *_: (layer, b, 0, 0))


def _ada_kernel(c_ref, w_ref, b_ref, o_ref):
    cond = _silu(c_ref[...])
    o_ref[0] = _dot(cond.astype(BF16), w_ref[0].astype(BF16)) + b_ref[0]


def _ada_call(c, w_ada, b_ada):
    depth, d, n = w_ada.shape
    bsz = c.shape[0]
    tn = n // 4
    return pl.pallas_call(
        _ada_kernel,
        grid=(depth, n // tn),
        in_specs=[pl.BlockSpec((bsz, d), lambda l, j: (0, 0)),
                  pl.BlockSpec((1, d, tn), lambda l, j: (l, 0, j)),
                  pl.BlockSpec((1, 1, tn), lambda l, j: (l, 0, j))],
        out_specs=pl.BlockSpec((1, bsz, tn), lambda l, j: (l, 0, j)),
        out_shape=jax.ShapeDtypeStruct((depth, bsz, n), F32),
        compiler_params=pltpu.CompilerParams(dimension_semantics=("arbitrary", "arbitrary"),
                                             vmem_limit_bytes=VMEM_LIMIT),
        name="adaln_mod",
    )(c, w_ada, b_ada.reshape(depth, 1, n))


def _pair_masks():
    t = lax.broadcasted_iota(I32, (SUPER, HEADS * SUPER), 0)
    s = lax.broadcasted_iota(I32, (SUPER, HEADS * SUPER), 1) & (SUPER - 1)
    masks = [((t >> LA_CHUNK_LOG2) == (s >> LA_CHUNK_LOG2)) & (s <= t)]
    for sh in range(LA_CHUNK_LOG2 + 1, SUPER_LOG2 + 1):
        half = 1 << (sh - 1)
        masks.append(((t >> sh) == (s >> sh)) & ((t & half) != 0) & ((s & half) == 0))
    return masks


def _level_refs(c):
    kt = c.shape[-1]
    n16 = SUPER >> LA_CHUNK_LOG2
    mid = (1 << LA_CHUNK_LOG2) // 2
    refs = [jnp.broadcast_to(c.reshape(n16, 1 << LA_CHUNK_LOG2, kt)[:, mid:mid + 1, :],
                             (n16, 1 << LA_CHUNK_LOG2, kt)).reshape(SUPER, kt)]
    for sh in range(LA_CHUNK_LOG2 + 1, SUPER_LOG2 + 1):
        nb, bs, half = SUPER >> sh, 1 << sh, 1 << (sh - 1)
        refs.append(jnp.broadcast_to(c.reshape(nb, bs, kt)[:, half - 1:half, :], (nb, bs, kt)).reshape(SUPER, kt))
    return refs


def _head_rows(a):
    w = a.shape[-1] // HEADS
    ab = a.astype(BF16)
    lane_head = lax.broadcasted_iota(I32, a.shape, 1) // w
    return jnp.concatenate([jnp.where(lane_head == h, ab, jnp.zeros_like(ab)) for h in range(HEADS)], axis=0)


def _gated_linear_attention_step(q, k, v, log_a, st_ref, ltri, masks):
    kt = q.shape[-1]
    hk = kt // HEADS
    a_hi, a_lo = _split_bf16(log_a, 2)
    c = _dot(ltri, a_hi) + _dot(ltri, a_lo)

    scores = None
    for lvl, (cref, mask) in enumerate(zip(_level_refs(c), masks)):
        d = c - cref
        if lvl == 0:
            eq, ek = jnp.exp(d), jnp.exp(-d)
        else:
            eq = ek = jnp.exp(-jnp.abs(d))
        s_lvl = _dot_nt((q * eq).astype(BF16), _head_rows(k * ek))
        scores = jnp.where(mask, s_lvl, 0.0 if scores is None else scores)
    o = _dot(scores.astype(BF16), _head_rows(v))

    st = st_ref[...]
    o = o + _dot_nt((q * jnp.exp(c)).astype(BF16), st.astype(BF16))
    c_last = c[SUPER - 1:SUPER, :]
    kd = (k * jnp.exp(c_last - c)).astype(BF16)
    upd = _dot_tn(v.astype(BF16), kd)
    diag = (lax.broadcasted_iota(I32, (GROUP_WIDTH, kt), 0) // HEAD_DIM) == (
        lax.broadcasted_iota(I32, (GROUP_WIDTH, kt), 1) // hk)
    st_ref[...] = st * jnp.exp(c_last) + jnp.where(diag, upd, 0.0)
    return o


def _head_rms_gate(o, head_mean, g, gate):
    ms = _dot((o * o).astype(BF16), head_mean)
    return (o * lax.rsqrt(ms + EPS)) * g * _silu(gate)


def _mixer_kernel(layer, hr, x_ref, mod_ref, ng_ref, win_ref, wout_ref, convw_ref, sgug_ref, sguw_ref, sgub_ref,
                  wgate_ref, bgate_ref, glag_ref, hlb_ref, hgng_ref, o_ref,
                  p0_ref, p1_ref, y0_ref, y1_ref, zbuf, st_gla, st_hgrn):
    @pl.when(pl.program_id(1) == 0)
    def _():
        zbuf[0:8, :] = jnp.zeros((8, GROUP_WIDTH), F32)
        st_gla[...] = jnp.zeros_like(st_gla)
        st_hgrn[...] = jnp.zeros_like(st_hgrn)

    mod = mod_ref[...]
    gain = ng_ref[...] * (1.0 + mod[1:2])
    halves = ((p0_ref, y0_ref), (p1_ref, y1_ref))

    for hf, (p_ref, _) in enumerate(halves):
        xt = x_ref[0, hf * hr:(hf + 1) * hr, :]
        ms = jnp.mean(xt * xt, axis=-1, keepdims=True)
        h = (xt * lax.rsqrt(ms + EPS)) * gain + mod[0:1]
        p_ref[...] = _dot(h.astype(BF16), win_ref[...])

    masks = _pair_masks()
    ri = lax.broadcasted_iota(I32, (SUPER, SUPER), 0)
    ci = lax.broadcasted_iota(I32, (SUPER, SUPER), 1)
    ltri = jnp.where(ci <= ri, 1.0, 0.0).astype(BF16)
    t_ws = lax.broadcasted_iota(I32, (SUPER, HEADS * SUPER), 0)
    s_ws = lax.broadcasted_iota(I32, (SUPER, HEADS * SUPER), 1) & (SUPER - 1)
    ws = jnp.where(s_ws <= t_ws, sguw_ref[...], 0.0).astype(BF16)
    hrow = lax.broadcasted_iota(I32, (GROUP_WIDTH, GROUP_WIDTH), 0) // HEAD_DIM
    hcol = lax.broadcasted_iota(I32, (GROUP_WIDTH, GROUP_WIDTH), 1) // HEAD_DIM
    head_mean = jnp.where(hrow == hcol, 1.0 / HEAD_DIM, 0.0).astype(BF16)
    group_mean = jnp.full((GROUP_WIDTH, GROUP_WIDTH), 1.0 / GROUP_WIDTH, BF16)
    lb_all = hlb_ref[...]
    lb_e = jnp.exp(lb_all - jnp.max(lb_all, axis=0, keepdims=True))
    lb_sm = lb_e / jnp.sum(lb_e, axis=0, keepdims=True)
    lb = jnp.zeros((1, GROUP_WIDTH), F32)
    for j in range(1, layer + 1):
        lb = lb + lb_sm[j:j + 1]
    cw = convw_ref[...]

    for hf, (p_ref, y_ref) in enumerate(halves):
        z = p_ref[:, C_CC:C_CC + GROUP_WIDTH] * p_ref[:, C_CX:C_CX + GROUP_WIDTH]
        zbuf[8:8 + hr, :] = z
        conv = cw[0:1] * zbuf[6:6 + hr, :] + cw[1:2] * zbuf[7:7 + hr, :] + cw[2:3] * z
        y_ref[:, 0:GROUP_WIDTH] = (p_ref[:, C_CB:C_CB + GROUP_WIDTH] * conv).astype(BF16)
        zbuf[0:8, :] = zbuf[hr:hr + 8, :]

        sv = p_ref[:, C_SV:C_SV + GROUP_WIDTH]
        sv_hi, sv_lo = _split_bf16(sv, 2)
        dv = sv - (_dot(sv_hi, group_mean) + _dot(sv_lo, group_mean))
        var = _dot((dv * dv).astype(BF16), group_mean)
        vn = (dv * lax.rsqrt(var + EPS)) * sgug_ref[...]
        gate_logit = _dot(p_ref[:, C_AG:C_AG + LANES].astype(BF16), wgate_ref[...]) + bgate_ref[...]
        log_a = (jnp.minimum(gate_logit, 0.0) - jnp.log(1.0 + jnp.exp(-jnp.abs(gate_logit)))) * (
            1.0 / GLA_GATE_NORMALIZER)
        f = lb + (1.0 - lb) * jax.nn.sigmoid(p_ref[:, C_HF:C_HF + GROUP_WIDTH])
        log_f = jnp.log(f)

        o_gla, o_hgrn = [], []
        for i in range(hr // SUPER):
            r = slice(i * SUPER, (i + 1) * SUPER)
            mixed = _dot(ws, _head_rows(vn[r])) + sgub_ref[...]
            y_ref[r, GROUP_WIDTH:2 * GROUP_WIDTH] = (p_ref[r, C_SU:C_SU + GROUP_WIDTH] * mixed).astype(BF16)
            o_gla.append(_gated_linear_attention_step(
                p_ref[r, C_AQ:C_AQ + LANES] * (GLA_KEY_DIM ** -0.5), p_ref[r, C_AK:C_AK + LANES],
                p_ref[r, C_AV:C_AV + GROUP_WIDTH], log_a[r], st_gla, ltri, masks))
            o_hgrn.append(_gated_linear_attention_step(
                p_ref[r, C_HQ:C_HQ + GROUP_WIDTH], 1.0 - f[r], p_ref[r, C_HI:C_HI + GROUP_WIDTH], log_f[r],
                st_hgrn, ltri, masks))
        y_ref[:, 2 * GROUP_WIDTH:3 * GROUP_WIDTH] = _head_rms_gate(
            jnp.concatenate(o_gla, axis=0), head_mean, glag_ref[...], p_ref[:, C_AR:C_AR + GROUP_WIDTH]).astype(BF16)
        y_ref[:, 3 * GROUP_WIDTH:4 * GROUP_WIDTH] = _head_rms_gate(
            jnp.concatenate(o_hgrn, axis=0), head_mean, hgng_ref[...], p_ref[:, C_HG:C_HG + GROUP_WIDTH]).astype(BF16)

        rows = slice(hf * hr, (hf + 1) * hr)
        o_ref[0, rows, :] = x_ref[0, rows, :] + mod[2:3] * _dot(y_ref[...], wout_ref[...])


def _mixer_call(layer, x, mod, ng, win, wout, convw, sgug, sguw, sgub, wgate, bgate, glag, hlb, hgng):
    bsz, seq, d = x.shape
    ts = MIXER_ROWS
    hr = ts // 2
    small = [convw, sgug, sguw, sgub, wgate, bgate, glag, hlb, hgng]
    return pl.pallas_call(
        functools.partial(_mixer_kernel, layer, hr),
        grid=(bsz, seq // ts),
        in_specs=[pl.BlockSpec((1, ts, d), lambda b, s: (b, s, 0)), _mod_spec(layer, d), _const_spec(ng.shape),
                  _layer_spec(win, layer), _layer_spec(wout, layer)] + [_const_spec(a.shape) for a in small],
        out_specs=pl.BlockSpec((1, ts, d), lambda b, s: (b, s, 0)),
        out_shape=jax.ShapeDtypeStruct(x.shape, F32),
        scratch_shapes=[pltpu.VMEM((hr, IN_PROJ_PAD), F32), pltpu.VMEM((hr, IN_PROJ_PAD), F32),
                        pltpu.VMEM((hr, d), BF16), pltpu.VMEM((hr, d), BF16),
                        pltpu.VMEM((hr + 8, GROUP_WIDTH), F32),
                        pltpu.VMEM((GROUP_WIDTH, HEADS * GLA_KEY_DIM), F32),
                        pltpu.VMEM((GROUP_WIDTH, GROUP_WIDTH), F32)],
        input_output_aliases={0: 0} if layer > 0 else {},
        compiler_params=pltpu.CompilerParams(dimension_semantics=("parallel", "arbitrary"),
                                             vmem_limit_bytes=VMEM_LIMIT),
        name=f"mixer_l{layer}",
    )(x, mod, ng, win, wout, *small)


def _final_norm(out, g):
    ms = jnp.mean(out * out, axis=-1, keepdims=True)
    return (out * lax.rsqrt(ms + EPS)) * g


def _ffn_kernel(final, x_ref, mod_ref, ng_ref, w1_ref, w3_ref, w2_ref, fng_ref, o_ref):
    xt = x_ref[0]
    mod = mod_ref[...]
    hb = _mod_norm(xt, ng_ref[...], mod[3:4], mod[4:5]).astype(BF16)
    h1 = _dot(hb, w1_ref[...])
    h3 = _dot(hb, w3_ref[...])
    a = (_silu(h1) * h3).astype(BF16)
    out = xt + mod[5:6] * _dot(a, w2_ref[...])
    o_ref[0] = _final_norm(out, fng_ref[...]) if final else out


def _ffn_call(final, layer, x, mod, ng, w1, w3, w2, fng):
    bsz, seq, d = x.shape
    tm = FFN_ROWS
    idx = layer // 2
    return pl.pallas_call(
        functools.partial(_ffn_kernel, final),
        grid=(bsz, seq // tm),
        in_specs=[pl.BlockSpec((1, tm, d), lambda b, s: (b, s, 0)), _mod_spec(layer, d), _const_spec(ng.shape),
                  _layer_spec(w1, idx), _layer_spec(w3, idx), _layer_spec(w2, idx), _const_spec(fng.shape)],
        out_specs=pl.BlockSpec((1, tm, d), lambda b, s: (b, s, 0)),
        out_shape=jax.ShapeDtypeStruct(x.shape, F32),
        input_output_aliases={0: 0},
        compiler_params=pltpu.CompilerParams(dimension_semantics=("parallel", "arbitrary"),
                                             vmem_limit_bytes=VMEM_LIMIT),
        name="dense_ffn",
    )(x, mod, ng, w1, w3, w2, fng)


SEG_SIZES = tuple(BF16_ROWS << i for i in range(5, -1, -1))


def _segment_copies(fn, buf_off, hbm_off, nrows):
    pos = jnp.int32(0)
    for size in SEG_SIZES:
        fn(pl.multiple_of(buf_off + pos, BF16_ROWS), pl.multiple_of(hbm_off + pos, BF16_ROWS), size,
           (nrows & size) != 0)
        pos = pos + (nrows & size)


def _route_kernel(nsb, nblk, x_ref, mod_ref, ng_ref, wr_ref, sut_ref, xs_hbm, route_ref, seg_ref,
                  xs_buf, run_ref, prev_ref, sem):
    tb, slots = MOE_TB, MOE_SLOTS
    blk = pl.program_id(0) * nsb + pl.program_id(1)

    @pl.when(blk == 0)
    def _():
        for e in range(N_EXPERTS):
            run_ref[e] = 0
        for i in range(3 * N_EXPERTS):
            prev_ref[i] = 0

    mod = mod_ref[...]
    h = _mod_norm(x_ref[0], ng_ref[...], mod[3:4], mod[4:5])
    h_hi, h_lo = _split_bf16(h, 2)
    lg = _dot(h_hi, wr_ref[...]) + _dot(h_lo, wr_ref[...])
    logits = lg[:, 0:LANES] + lg[:, LANES:2 * LANES]
    lt = logits.T[0:N_EXPERTS, :]
    eid = lax.broadcasted_iota(I32, (N_EXPERTS, tb), 0)
    m1 = jnp.max(lt, axis=0, keepdims=True)
    i1 = jnp.min(jnp.where(lt == m1, eid, N_EXPERTS), axis=0, keepdims=True)
    lt2 = jnp.where(eid == i1, -jnp.inf, lt)
    m2 = jnp.max(lt2, axis=0, keepdims=True)
    i2 = jnp.min(jnp.where(lt2 == m2, eid, N_EXPERTS), axis=0, keepdims=True)
    e2 = jnp.exp(m2 - m1)
    w_first = 1.0 / (1.0 + e2)
    w_second = e2 / (1.0 + e2)

    sel1, sel2 = eid == i1, eid == i2
    onehot = jnp.where(sel1 | sel2, 1.0, 0.0).astype(F32)
    rank = _dot(onehot.astype(BF16), sut_ref[...])
    counts = [jnp.sum(onehot[e:e + 1, :]).astype(I32) for e in range(N_EXPERTS)]
    cnt16 = [((n + (BF16_ROWS - 1)) >> BF16_ROWS_LOG2) << BF16_ROWS_LOG2 for n in counts]
    offs, total = [], jnp.int32(0)
    for e in range(N_EXPERTS):
        offs.append(total)
        total = total + cnt16[e]
    runs = [run_ref[e] for e in range(N_EXPERTS)]

    sub = lax.broadcasted_iota(I32, (N_EXPERTS, 1), 0)

    def expert_column(vals):
        col = jnp.zeros((N_EXPERTS, 1), I32)
        for e in range(N_EXPERTS):
            col = jnp.where(sub == e, vals[e], col)
        return col

    off_col, cnt_col, run_col = expert_column(offs), expert_column(cnt16), expert_column(runs)
    slot_all = off_col.astype(F32) + rank
    slot1 = jnp.sum(jnp.where(sel1, slot_all, 0.0), axis=0, keepdims=True)
    slot2 = jnp.sum(jnp.where(sel2, slot_all, 0.0), axis=0, keepdims=True)

    rowi = lax.broadcasted_iota(I32, (slots, tb), 0)
    perm = jnp.where((rowi == slot1.astype(I32)) | (rowi == slot2.astype(I32)), 1.0, 0.0).astype(BF16)
    xs_sorted = _dot(perm, h_hi).astype(BF16)

    def copy(e, buf_row, hbm_row, size):
        return pltpu.make_async_copy(xs_buf.at[pl.ds(buf_row, size), :], xs_hbm.at[e, pl.ds(hbm_row, size), :], sem)

    def start(e):
        def fn(buf_row, hbm_row, size, cond):
            @pl.when(cond)
            def _():
                copy(e, buf_row, hbm_row, size).start()
        return fn

    def wait(e):
        def fn(buf_row, hbm_row, size, cond):
            @pl.when(cond)
            def _():
                copy(e, buf_row, hbm_row, size).wait()
        return fn

    for e in range(N_EXPERTS):
        _segment_copies(wait(e), prev_ref[e], prev_ref[2 * N_EXPERTS + e], prev_ref[N_EXPERTS + e])
    xs_buf[...] = xs_sorted
    for e in range(N_EXPERTS):
        _segment_copies(start(e), offs[e], runs[e], cnt16[e])

    @pl.when(blk == nblk - 1)
    def _():
        for e in range(N_EXPERTS):
            _segment_copies(wait(e), offs[e], runs[e], cnt16[e])

    route_ref[0] = jnp.concatenate([slot1, slot2, w_first, w_second, jnp.zeros((4, tb), F32)], axis=0)
    lane = lax.broadcasted_iota(I32, (N_EXPERTS, LANES), 1)
    seg_ref[0] = jnp.where(lane == 0, off_col, jnp.where(lane == 1, cnt_col, jnp.where(lane == 2, run_col, 0)))
    for e in range(N_EXPERTS):
        run_ref[e] = runs[e] + cnt16[e]
        prev_ref[e] = offs[e]
        prev_ref[N_EXPERTS + e] = cnt16[e]
        prev_ref[2 * N_EXPERTS + e] = runs[e]


def _route_call(layer, x, mod, ng, wr2, cap):
    bsz, seq, d = x.shape
    tb = MOE_TB
    nsb = seq // tb
    nblk = bsz * nsb
    ri = lax.broadcasted_iota(I32, (tb, tb), 0)
    ci = lax.broadcasted_iota(I32, (tb, tb), 1)
    sut = (ri < ci).astype(BF16)
    return pl.pallas_call(
        functools.partial(_route_kernel, nsb, nblk),
        grid=(bsz, nsb),
        in_specs=[pl.BlockSpec((1, tb, d), lambda b, s: (b, s, 0)), _mod_spec(layer, d),
                  _const_spec(ng.shape), _const_spec(wr2.shape), _const_spec((tb, tb))],
        out_specs=[pl.BlockSpec(memory_space=pl.ANY),
                   pl.BlockSpec((1, 8, tb), lambda b, s: (b * nsb + s, 0, 0)),
                   pl.BlockSpec((1, N_EXPERTS, LANES), lambda b, s: (b * nsb + s, 0, 0))],
        out_shape=[jax.ShapeDtypeStruct((N_EXPERTS, cap, d), BF16),
                   jax.ShapeDtypeStruct((nblk, 8, tb), F32),
                   jax.ShapeDtypeStruct((nblk, N_EXPERTS, LANES), I32)],
        scratch_shapes=[pltpu.VMEM((MOE_SLOTS, d), BF16), pltpu.SMEM((N_EXPERTS,), I32),
                        pltpu.SMEM((3 * N_EXPERTS,), I32), pltpu.SemaphoreType.DMA],
        compiler_params=pltpu.CompilerParams(dimension_semantics=("arbitrary", "arbitrary"),
                                             vmem_limit_bytes=VMEM_LIMIT),
        name="moe_route",
    )(x, mod, ng, wr2, sut)


def _gmm_kernel(te_ref, trb_ref, tval_ref, tnv_ref, xs_ref, w1_ref, w3_ref, w2_ref, ys_ref):
    i = pl.program_id(0)

    @pl.when(tval_ref[i] == 1)
    def _():
        tm = xs_ref.shape[0]
        rows = lax.broadcasted_iota(I32, (tm, 1), 0)
        xs = xs_ref[...]
        xs = jnp.where(rows < tnv_ref[i], xs, jnp.zeros_like(xs))
        acc = None
        for j in range(w1_ref.shape[-1] // MOE_FF_CHUNK):
            cols = slice(j * MOE_FF_CHUNK, (j + 1) * MOE_FF_CHUNK)
            h1 = _dot(xs, w1_ref[:, cols])
            h3 = _dot(xs, w3_ref[:, cols])
            part = _dot((_silu(h1) * h3).astype(BF16), w2_ref[cols, :])
            acc = part if acc is None else acc + part
        ys_ref[...] = acc.astype(BF16)


def _gmm_call(idx, xs, w1, w3, w2, te, trb, tval, tnv):
    ne, cap, d = xs.shape
    tm = MOE_TILE_ROWS
    nt = te.shape[0]

    def wspec(w):
        return pl.BlockSpec((None, None) + w.shape[2:], lambda i, te, trb, tval, tnv: (idx, te[i], 0, 0),
                            pipeline_mode=pl.Buffered(1))

    return pl.pallas_call(
        _gmm_kernel,
        grid_spec=pltpu.PrefetchScalarGridSpec(
            num_scalar_prefetch=4,
            grid=(nt,),
            in_specs=[pl.BlockSpec((None, tm, d), lambda i, te, trb, tval, tnv: (te[i], trb[i], 0)),
                      wspec(w1), wspec(w3), wspec(w2)],
            out_specs=pl.BlockSpec((None, tm, d), lambda i, te, trb, tval, tnv: (te[i], trb[i], 0))),
        out_shape=jax.ShapeDtypeStruct((ne, cap, d), BF16),
        compiler_params=pltpu.CompilerParams(dimension_semantics=("arbitrary",), vmem_limit_bytes=VMEM_LIMIT),
        name="moe_experts",
    )(te, trb, tval, tnv, xs, w1, w3, w2)


def _combine_kernel(final, nsb, nblk, seg_ref, x_ref, mod_ref, route_ref, fng_ref, ys_hbm, o_ref, ys_buf, sem):
    tb, slots = MOE_TB, MOE_SLOTS
    blk = pl.program_id(0) * nsb + pl.program_id(1)
    cur = blk & 1

    def segments(b):
        base = b * (3 * N_EXPERTS)
        return ([seg_ref[base + e] for e in range(N_EXPERTS)],
                [seg_ref[base + N_EXPERTS + e] for e in range(N_EXPERTS)],
                [seg_ref[base + 2 * N_EXPERTS + e] for e in range(N_EXPERTS)])

    def fetch(b, buf, also, wait):
        offs, cnt16, runs = segments(b)
        for e in range(N_EXPERTS):
            def fn(buf_row, hbm_row, size, cond, e=e):
                @pl.when(cond & also)
                def _():
                    cp = pltpu.make_async_copy(ys_hbm.at[e, pl.ds(hbm_row, size), :],
                                               ys_buf.at[buf, pl.ds(buf_row, size), :], sem.at[buf])
                    cp.wait() if wait else cp.start()
            _segment_copies(fn, offs[e], runs[e], cnt16[e])
        return offs[N_EXPERTS - 1] + cnt16[N_EXPERTS - 1]

    @pl.when(blk == 0)
    def _():
        fetch(blk, cur, True, wait=False)

    fetch(jnp.minimum(blk + 1, nblk - 1), 1 - cur, blk + 1 < nblk, wait=False)
    total = fetch(blk, cur, True, wait=True)

    rowi = lax.broadcasted_iota(I32, (slots, 1), 0)
    ys = ys_buf[cur]
    ys = jnp.where(rowi < total, ys, jnp.zeros_like(ys))

    route = jnp.concatenate([route_ref[0], jnp.zeros((LANES - 8, tb), F32)], axis=0).T
    slot1, slot2 = route[:, 0:1].astype(I32), route[:, 1:2].astype(I32)
    lane = lax.broadcasted_iota(I32, (tb, slots), 1)
    comb = (jnp.where(lane == slot1, route[:, 2:3], 0.0) + jnp.where(lane == slot2, route[:, 3:4], 0.0)).astype(BF16)
    out = x_ref[0] + mod_ref[...][5:6] * _dot(comb, ys)
    o_ref[0] = _final_norm(out, fng_ref[...]) if final else out


def _combine_call(final, layer, x, mod, route, seg_flat, ys, fng):
    bsz, seq, d = x.shape
    tb = MOE_TB
    nsb = seq // tb
    return pl.pallas_call(
        functools.partial(_combine_kernel, final, nsb, bsz * nsb),
        grid_spec=pltpu.PrefetchScalarGridSpec(
            num_scalar_prefetch=1,
            grid=(bsz, nsb),
            in_specs=[pl.BlockSpec((1, tb, d), lambda b, s, seg: (b, s, 0)), _mod_spec(layer, d),
                      pl.BlockSpec((1, 8, tb), lambda b, s, seg: (b * nsb + s, 0, 0)),
                      _const_spec(fng.shape),
                      pl.BlockSpec(memory_space=pl.ANY)],
            out_specs=pl.BlockSpec((1, tb, d), lambda b, s, seg: (b, s, 0)),
            scratch_shapes=[pltpu.VMEM((2, MOE_SLOTS, d), BF16), pltpu.SemaphoreType.DMA((2,))]),
        out_shape=jax.ShapeDtypeStruct(x.shape, F32),
        input_output_aliases={1: 0},
        compiler_params=pltpu.CompilerParams(dimension_semantics=("arbitrary", "arbitrary"),
                                             vmem_limit_bytes=VMEM_LIMIT),
        name="moe_combine",
    )(seg_flat, x, mod, route, fng, ys)


def _moe_tiles(seg, tm, nt):
    total = seg[-1, :, 2] + seg[-1, :, 1]
    tiles = (total + tm - 1) // tm
    ends = jnp.cumsum(tiles)
    starts = ends - tiles
    ntiles = ends[-1]
    i = jnp.arange(nt, dtype=I32)
    valid = i < ntiles
    ie = jnp.minimum(i, ntiles - 1)
    te = jnp.minimum(jnp.sum((ie[:, None] >= ends[None, :]).astype(I32), axis=1), N_EXPERTS - 1)
    trb = ie - starts[te]
    tnv = jnp.clip(total[te] - trb * tm, 0, tm)
    return te.astype(I32), trb.astype(I32), valid.astype(I32), tnv.astype(I32)


def _moe_layer(final, layer, x, mod, ng, wr2, w1, w3, w2, fng):
    bsz, seq, d = x.shape
    tokens = bsz * seq
    nblk = tokens // MOE_TB
    tm = MOE_TILE_ROWS
    cap = -(-(tokens + (BF16_ROWS - 1) * nblk) // tm) * tm
    nt = (2 * tokens + (BF16_ROWS - 1) * N_EXPERTS * nblk) // tm + N_EXPERTS
    xs, route, seg = _route_call(layer, x, mod, ng, wr2, cap)
    te, trb, tval, tnv = _moe_tiles(seg, tm, nt)
    ys = _gmm_call(layer // 2, xs, w1, w3, w2, te, trb, tval, tnv)
    seg_flat = jnp.transpose(seg[:, :, 0:3], (0, 2, 1)).reshape(-1)
    return _combine_call(final, layer, x, mod, route, seg_flat, ys, fng)


def kernel(x, c, norm_mix_g, norm_ffn_g, final_norm_g, w_ada, b_ada, w_in, w_out, conv_w, sgu_norm_g, sgu_w, sgu_b,
           gla_w_gate, gla_b_gate, gla_norm_g, hgrn_lower_bounds, hgrn_norm_g, ffn_w1, ffn_w3, ffn_w2, moe_router,
           moe_w1, moe_w3, moe_w2):
    depth = w_in.shape[0]
    bsz, seq, d = x.shape
    assert d == D_MODEL and seq % MOE_TB == 0 and hgrn_lower_bounds.shape[0] == depth

    mod = _ada_call(c, w_ada, b_ada).reshape(depth, bsz, N_MOD, d)

    win = jnp.concatenate([w_in[:, :, :IN_PROJ_SPLIT],
                           jnp.zeros((depth, d, LANES - GLA_GATE_RANK), w_in.dtype),
                           w_in[:, :, IN_PROJ_SPLIT:]], axis=-1).astype(BF16)
    wout = w_out.astype(BF16)
    wgate = jnp.concatenate([gla_w_gate, jnp.zeros((depth, LANES - GLA_GATE_RANK, gla_w_gate.shape[-1]), F32)],
                            axis=1).astype(BF16)
    fng = final_norm_g.reshape(1, d)
    ffn_w = [w.astype(BF16) for w in (ffn_w1, ffn_w3, ffn_w2)]
    moe_w = [w.astype(BF16) for w in (moe_w1, moe_w3, moe_w2)]
    wr = jnp.concatenate([moe_router, jnp.zeros(moe_router.shape[:2] + (LANES - N_EXPERTS,), F32)], axis=-1)
    wr_hi = wr.astype(BF16)
    wr2 = jnp.concatenate([wr_hi, (wr - wr_hi.astype(F32)).astype(BF16)], axis=-1)

    for layer in range(depth):
        x = _mixer_call(
            layer, x, mod, norm_mix_g[layer].reshape(1, d), win, wout, conv_w[layer],
            sgu_norm_g[layer].reshape(1, GROUP_WIDTH),
            jnp.transpose(sgu_w[layer], (1, 0, 2)).reshape(SUPER, HEADS * SUPER),
            jnp.repeat(sgu_b[layer].T, HEAD_DIM, axis=1), wgate[layer], gla_b_gate[layer].reshape(1, -1),
            jnp.tile(gla_norm_g[layer], HEADS).reshape(1, GROUP_WIDTH), hgrn_lower_bounds,
            jnp.tile(hgrn_norm_g[layer], HEADS).reshape(1, GROUP_WIDTH))
        final = layer == depth - 1
        ng = norm_ffn_g[layer].reshape(1, d)
        if layer % 2 == 0:
            x = _ffn_call(final, layer, x, mod, ng, *ffn_w, fng)
        else:
            x = _moe_layer(final, layer, x, mod, ng, wr2[layer // 2], *moe_w, fng)
    return x
```

```python
import functools

import jax
import jax.numpy as jnp
from jax import lax
from jax.experimental import pallas as pl
from jax.experimental.pallas import tpu as pltpu

F32 = jnp.float32
BF16 = jnp.bfloat16
I32 = jnp.int32
EPS = 1e-6

D_MODEL = 1024
N_MOD = 6
GROUP_WIDTH = 256
HEADS = 4
HEAD_DIM = 64
GLA_KEY_DIM = 32
GLA_GATE_RANK = 16
GLA_GATE_NORMALIZER = 16.0
LA_CHUNK_LOG2 = 4
SUPER = 128
SUPER_LOG2 = 7
N_EXPERTS = 8
LANES = 128
BF16_ROWS = 16
BF16_ROWS_LOG2 = 4

C_CB, C_CC, C_CX, C_SU, C_SV = 0, 256, 512, 768, 1024
C_AQ, C_AK, C_AV, C_AG, C_AR = 1280, 1408, 1536, 1792, 1920
C_HQ, C_HF, C_HI, C_HG = 2176, 2432, 2688, 2944
IN_PROJ_PAD = 3200
IN_PROJ_SPLIT = 1808

VMEM_LIMIT = 56 * 1024 * 1024

MIXER_ROWS = 1024
FFN_ROWS = 512
MOE_TB = 512
MOE_SLOTS = 2 * MOE_TB + 2 * LANES
MOE_TILE_ROWS = 512
MOE_FF_CHUNK = 512


def _silu(x):
    return x * jax.nn.sigmoid(x)


def _dot(a, b):
    return jnp.dot(a, b, preferred_element_type=F32)


def _dot_nt(a, b):
    return lax.dot_general(a, b, (((1,), (1,)), ((), ())), preferred_element_type=F32)


def _dot_tn(a, b):
    return lax.dot_general(a, b, (((0,), (0,)), ((), ())), preferred_element_type=F32)


def _split_bf16(x, pieces):
    out, r = [], x
    for _ in range(pieces):
        t = r.astype(BF16)
        out.append(t)
        r = r - t.astype(F32)
    return out


def _mod_norm(xt, g, shift, scale):
    ms = jnp.mean(xt * xt, axis=-1, keepdims=True)
    return (xt * lax.rsqrt(ms + EPS)) * g * (1.0 + scale) + shift


def _const_spec(shape):
    zeros = (0,) * len(shape)
    return pl.BlockSpec(shape, lambda *_: zeros, pipeline_mode=pl.Buffered(1))


def _layer_spec(arr, layer):
    zeros = (0,) * (arr.ndim - 1)
    return pl.BlockSpec((None,) + arr.shape[1:], lambda *_: (layer,) + zeros, pipeline_mode=pl.Buffered(1))


def _mod_spec(layer, d):
    return pl.BlockSpec((None, None, N_MOD, d), lambda b, s, *_: (layer, b, 0, 0))


def _ada_kernel(c_ref, w_ref, b_ref, o_ref):
    cond = _silu(c_ref[...])
    o_ref[0] = _dot(cond.astype(BF16), w_ref[0].astype(BF16)) + b_ref[0]


def _ada_call(c, w_ada, b_ada):
    depth, d, n = w_ada.shape
    bsz = c.shape[0]
    tn = n // 4
    return pl.pallas_call(
        _ada_kernel,
        grid=(depth, n // tn),
        in_specs=[pl.BlockSpec((bsz, d), lambda l, j: (0, 0)),
                  pl.BlockSpec((1, d, tn), lambda l, j: (l, 0, j)),
                  pl.BlockSpec((1, 1, tn), lambda l, j: (l, 0, j))],
        out_specs=pl.BlockSpec((1, bsz, tn), lambda l, j: (l, 0, j)),
        out_shape=jax.ShapeDtypeStruct((depth, bsz, n), F32),
        compiler_params=pltpu.CompilerParams(dimension_semantics=("arbitrary", "arbitrary"),
                                             vmem_limit_bytes=VMEM_LIMIT),
        name="adaln_mod",
    )(c, w_ada, b_ada.reshape(depth, 1, n))


def _pair_masks():
    t = lax.broadcasted_iota(I32, (SUPER, HEADS * SUPER), 0)
    s = lax.broadcasted_iota(I32, (SUPER, HEADS * SUPER), 1) & (SUPER - 1)
    masks = [((t >> LA_CHUNK_LOG2) == (s >> LA_CHUNK_LOG2)) & (s <= t)]
    for sh in range(LA_CHUNK_LOG2 + 1, SUPER_LOG2 + 1):
        half = 1 << (sh - 1)
        masks.append(((t >> sh) == (s >> sh)) & ((t & half) != 0) & ((s & half) == 0))
    return masks


def _level_refs(c):
    kt = c.shape[-1]
    n16 = SUPER >> LA_CHUNK_LOG2
    mid = (1 << LA_CHUNK_LOG2) // 2
    refs = [jnp.broadcast_to(c.reshape(n16, 1 << LA_CHUNK_LOG2, kt)[:, mid:mid + 1, :],
                             (n16, 1 << LA_CHUNK_LOG2, kt)).reshape(SUPER, kt)]
    for sh in range(LA_CHUNK_LOG2 + 1, SUPER_LOG2 + 1):
        nb, bs, half = SUPER >> sh, 1 << sh, 1 << (sh - 1)
        refs.append(jnp.broadcast_to(c.reshape(nb, bs, kt)[:, half - 1:half, :], (nb, bs, kt)).reshape(SUPER, kt))
    return refs


def _head_rows(a):
    w = a.shape[-1] // HEADS
    ab = a.astype(BF16)
    lane_head = lax.broadcasted_iota(I32, a.shape, 1) // w
    return jnp.concatenate([jnp.where(lane_head == h, ab, jnp.zeros_like(ab)) for h in range(HEADS)], axis=0)


def _gated_linear_attention_step(q, k, v, log_a, st_ref, ltri, masks):
    kt = q.shape[-1]
    hk = kt // HEADS
    a_hi, a_lo = _split_bf16(log_a, 2)
    c = _dot(ltri, a_hi) + _dot(ltri, a_lo)

    scores = None
    for lvl, (cref, mask) in enumerate(zip(_level_refs(c), masks)):
        d = c - cref
        if lvl == 0:
            eq, ek = jnp.exp(d), jnp.exp(-d)
        else:
            eq = ek = jnp.exp(-jnp.abs(d))
        s_lvl = _dot_nt((q * eq).astype(BF16), _head_rows(k * ek))
        scores = jnp.where(mask, s_lvl, 0.0 if scores is None else scores)
    o = _dot(scores.astype(BF16), _head_rows(v))

    st = st_ref[...]
    o = o + _dot_nt((q * jnp.exp(c)).astype(BF16), st.astype(BF16))
    c_last = c[SUPER - 1:SUPER, :]
    kd = (k * jnp.exp(c_last - c)).astype(BF16)
    upd = _dot_tn(v.astype(BF16), kd)
    diag = (lax.broadcasted_iota(I32, (GROUP_WIDTH, kt), 0) // HEAD_DIM) == (
        lax.broadcasted_iota(I32, (GROUP_WIDTH, kt), 1) // hk)
    st_ref[...] = st * jnp.exp(c_last) + jnp.where(diag, upd, 0.0)
    return o


def _head_rms_gate(o, head_mean, g, gate):
    ms = _dot((o * o).astype(BF16), head_mean)
    return (o * lax.rsqrt(ms + EPS)) * g * _silu(gate)


def _mixer_kernel(layer, hr, x_ref, mod_ref, ng_ref, win_ref, wout_ref, convw_ref, sgug_ref, sguw_ref, sgub_ref,
                  wgate_ref, bgate_ref, glag_ref, hlb_ref, hgng_ref, o_ref,
                  p0_ref, p1_ref, y0_ref, y1_ref, zbuf, st_gla, st_hgrn):
    @pl.when(pl.program_id(1) == 0)
    def _():
        zbuf[0:8, :] = jnp.zeros((8, GROUP_WIDTH), F32)
        st_gla[...] = jnp.zeros_like(st_gla)
        st_hgrn[...] = jnp.zeros_like(st_hgrn)

    mod = mod_ref[...]
    gain = ng_ref[...] * (1.0 + mod[1:2])
    halves = ((p0_ref, y0_ref), (p1_ref, y1_ref))

    for hf, (p_ref, _) in enumerate(halves):
        xt = x_ref[0, hf * hr:(hf + 1) * hr, :]
        ms = jnp.mean(xt * xt, axis=-1, keepdims=True)
        h = (xt * lax.rsqrt(ms + EPS)) * gain + mod[0:1]
        p_ref[...] = _dot(h.astype(BF16), win_ref[...])

    masks = _pair_masks()
    ri = lax.broadcasted_iota(I32, (SUPER, SUPER), 0)
    ci = lax.broadcasted_iota(I32, (SUPER, SUPER), 1)
    ltri = jnp.where(ci <= ri, 1.0, 0.0).astype(BF16)
    t_ws = lax.broadcasted_iota(I32, (SUPER, HEADS * SUPER), 0)
    s_ws = lax.broadcasted_iota(I32, (SUPER, HEADS * SUPER), 1) & (SUPER - 1)
    ws = jnp.where(s_ws <= t_ws, sguw_ref[...], 0.0).astype(BF16)
    hrow = lax.broadcasted_iota(I32, (GROUP_WIDTH, GROUP_WIDTH), 0) // HEAD_DIM
    hcol = lax.broadcasted_iota(I32, (GROUP_WIDTH, GROUP_WIDTH), 1) // HEAD_DIM
    head_mean = jnp.where(hrow == hcol, 1.0 / HEAD_DIM, 0.0).astype(BF16)
    group_mean = jnp.full((GROUP_WIDTH, GROUP_WIDTH), 1.0 / GROUP_WIDTH, BF16)
    lb_all = hlb_ref[...]
    lb_e = jnp.exp(lb_all - jnp.max(lb_all, axis=0, keepdims=True))
    lb_sm = lb_e / jnp.sum(lb_e, axis=0, keepdims=True)
    lb = jnp.zeros((1, GROUP_WIDTH), F32)
    for j in range(1, layer + 1):
        lb = lb + lb_sm[j:j + 1]
    cw = convw_ref[...]

    for hf, (p_ref, y_ref) in enumerate(halves):
        z = p_ref[:, C_CC:C_CC + GROUP_WIDTH] * p_ref[:, C_CX:C_CX + GROUP_WIDTH]
        zbuf[8:8 + hr, :] = z
        conv = cw[0:1] * zbuf[6:6 + hr, :] + cw[1:2] * zbuf[7:7 + hr, :] + cw[2:3] * z
        y_ref[:, 0:GROUP_WIDTH] = (p_ref[:, C_CB:C_CB + GROUP_WIDTH] * conv).astype(BF16)
        zbuf[0:8, :] = zbuf[hr:hr + 8, :]

        sv = p_ref[:, C_SV:C_SV + GROUP_WIDTH]
        sv_hi, sv_lo = _split_bf16(sv, 2)
        dv = sv - (_dot(sv_hi, group_mean) + _dot(sv_lo, group_mean))
        var = _dot((dv * dv).astype(BF16), group_mean)
        vn = (dv * lax.rsqrt(var + EPS)) * sgug_ref[...]
        gate_logit = _dot(p_ref[:, C_AG:C_AG + LANES].astype(BF16), wgate_ref[...]) + bgate_ref[...]
        log_a = (jnp.minimum(gate_logit, 0.0) - jnp.log(1.0 + jnp.exp(-jnp.abs(gate_logit)))) * (
            1.0 / GLA_GATE_NORMALIZER)
        f = lb + (1.0 - lb) * jax.nn.sigmoid(p_ref[:, C_HF:C_HF + GROUP_WIDTH])
        log_f = jnp.log(f)

        o_gla, o_hgrn = [], []
        for i in range(hr // SUPER):
            r = slice(i * SUPER, (i + 1) * SUPER)
            mixed = _dot(ws, _head_rows(vn[r])) + sgub_ref[...]
            y_ref[r, GROUP_WIDTH:2 * GROUP_WIDTH] = (p_ref[r, C_SU:C_SU + GROUP_WIDTH] * mixed).astype(BF16)
            o_gla.append(_gated_linear_attention_step(
                p_ref[r, C_AQ:C_AQ + LANES] * (GLA_KEY_DIM ** -0.5), p_ref[r, C_AK:C_AK + LANES],
                p_ref[r, C_AV:C_AV + GROUP_WIDTH], log_a[r], st_gla, ltri, masks))
            o_hgrn.append(_gated_linear_attention_step(
                p_ref[r, C_HQ:C_HQ + GROUP_WIDTH], 1.0 - f[r], p_ref[r, C_HI:C_HI + GROUP_WIDTH], log_f[r],
                st_hgrn, ltri, masks))
        y_ref[:, 2 * GROUP_WIDTH:3 * GROUP_WIDTH] = _head_rms_gate(
            jnp.concatenate(o_gla, axis=0), head_mean, glag_ref[...], p_ref[:, C_AR:C_AR + GROUP_WIDTH]).astype(BF16)
        y_ref[:, 3 * GROUP_WIDTH:4 * GROUP_WIDTH] = _head_rms_gate(
            jnp.concatenate(o_hgrn, axis=0), head_mean, hgng_ref[...], p_ref[:, C_HG:C_HG + GROUP_WIDTH]).astype(BF16)

        rows = slice(hf * hr, (hf + 1) * hr)
        o_ref[0, rows, :] = x_ref[0, rows, :] + mod[2:3] * _dot(y_ref[...], wout_ref[...])


def _mixer_call(layer, x, mod, ng, win, wout, convw, sgug, sguw, sgub, wgate, bgate, glag, hlb, hgng):
    bsz, seq, d = x.shape
    ts = MIXER_ROWS
    hr = ts // 2
    small = [convw, sgug, sguw, sgub, wgate, bgate, glag, hlb, hgng]
    return pl.pallas_call(
        functools.partial(_mixer_kernel, layer, hr),
        grid=(bsz, seq // ts),
        in_specs=[pl.BlockSpec((1, ts, d), lambda b, s: (b, s, 0)), _mod_spec(layer, d), _const_spec(ng.shape),
                  _layer_spec(win, layer), _layer_spec(wout, layer)] + [_const_spec(a.shape) for a in small],
        out_specs=pl.BlockSpec((1, ts, d), lambda b, s: (b, s, 0)),
        out_shape=jax.ShapeDtypeStruct(x.shape, F32),
        scratch_shapes=[pltpu.VMEM((hr, IN_PROJ_PAD), F32), pltpu.VMEM((hr, IN_PROJ_PAD), F32),
                        pltpu.VMEM((hr, d), BF16), pltpu.VMEM((hr, d), BF16),
                        pltpu.VMEM((hr + 8, GROUP_WIDTH), F32),
                        pltpu.VMEM((GROUP_WIDTH, HEADS * GLA_KEY_DIM), F32),
                        pltpu.VMEM((GROUP_WIDTH, GROUP_WIDTH), F32)],
        input_output_aliases={0: 0} if layer > 0 else {},
        compiler_params=pltpu.CompilerParams(dimension_semantics=("parallel", "arbitrary"),
                                             vmem_limit_bytes=VMEM_LIMIT),
        name=f"mixer_l{layer}",
    )(x, mod, ng, win, wout, *small)


def _final_norm(out, g):
    ms = jnp.mean(out * out, axis=-1, keepdims=True)
    return (out * lax.rsqrt(ms + EPS)) * g


def _ffn_kernel(final, x_ref, mod_ref, ng_ref, w1_ref, w3_ref, w2_ref, fng_ref, o_ref):
    xt = x_ref[0]
    mod = mod_ref[...]
    hb = _mod_norm(xt, ng_ref[...], mod[3:4], mod[4:5]).astype(BF16)
    h1 = _dot(hb, w1_ref[...])
    h3 = _dot(hb, w3_ref[...])
    a = (_silu(h1) * h3).astype(BF16)
    out = xt + mod[5:6] * _dot(a, w2_ref[...])
    o_ref[0] = _final_norm(out, fng_ref[...]) if final else out


def _ffn_call(final, layer, x, mod, ng, w1, w3, w2, fng):
    bsz, seq, d = x.shape
    tm = FFN_ROWS
    idx = layer // 2
    return pl.pallas_call(
        functools.partial(_ffn_kernel, final),
        grid=(bsz, seq // tm),
        in_specs=[pl.BlockSpec((1, tm, d), lambda b, s: (b, s, 0)), _mod_spec(layer, d), _const_spec(ng.shape),
                  _layer_spec(w1, idx), _layer_spec(w3, idx), _layer_spec(w2, idx), _const_spec(fng.shape)],
        out_specs=pl.BlockSpec((1, tm, d), lambda b, s: (b, s, 0)),
        out_shape=jax.ShapeDtypeStruct(x.shape, F32),
        input_output_aliases={0: 0},
        compiler_params=pltpu.CompilerParams(dimension_semantics=("parallel", "arbitrary"),
                                             vmem_limit_bytes=VMEM_LIMIT),
        name="dense_ffn",
    )(x, mod, ng, w1, w3, w2, fng)


SEG_SIZES = tuple(BF16_ROWS << i for i in range(5, -1, -1))


def _segment_copies(fn, buf_off, hbm_off, nrows):
    pos = jnp.int32(0)
    for size in SEG_SIZES:
        fn(pl.multiple_of(buf_off + pos, BF16_ROWS), pl.multiple_of(hbm_off + pos, BF16_ROWS), size,
           (nrows & size) != 0)
        pos = pos + (nrows & size)


def _route_kernel(nsb, nblk, x_ref, mod_ref, ng_ref, wr_ref, sut_ref, xs_hbm, route_ref, seg_ref,
                  xs_buf, carry_buf, run_ref, prev_ref, sem):
    tb, slots = MOE_TB, MOE_SLOTS
    blk = pl.program_id(0) * nsb + pl.program_id(1)

    @pl.when(blk == 0)
    def _():
        carry_buf[...] = jnp.zeros_like(carry_buf)
        for e in range(N_EXPERTS):
            run_ref[e] = 0
        for i in range(3 * N_EXPERTS):
            prev_ref[i] = 0

    mod = mod_ref[...]
    h = _mod_norm(x_ref[0], ng_ref[...], mod[3:4], mod[4:5])
    h_hi, h_lo = _split_bf16(h, 2)
    lg = _dot(h_hi, wr_ref[...]) + _dot(h_lo, wr_ref[...])
    logits = lg[:, 0:LANES] + lg[:, LANES:2 * LANES]
    lt = logits.T[0:N_EXPERTS, :]
    eid = lax.broadcasted_iota(I32, (N_EXPERTS, tb), 0)
    m1 = jnp.max(lt, axis=0, keepdims=True)
    i1 = jnp.min(jnp.where(lt == m1, eid, N_EXPERTS), axis=0, keepdims=True)
    lt2 = jnp.where(eid == i1, -jnp.inf, lt)
    m2 = jnp.max(lt2, axis=0, keepdims=True)
    i2 = jnp.min(jnp.where(lt2 == m2, eid, N_EXPERTS), axis=0, keepdims=True)
    e2 = jnp.exp(m2 - m1)
    w_first = 1.0 / (1.0 + e2)
    w_second = e2 / (1.0 + e2)

    sel1, sel2 = eid == i1, eid == i2
    onehot = jnp.where(sel1 | sel2, 1.0, 0.0).astype(F32)
    rank = _dot(onehot.astype(BF16), sut_ref[...])
    counts = [jnp.sum(onehot[e:e + 1, :]).astype(I32) for e in range(N_EXPERTS)]
    runs = [run_ref[e] for e in range(N_EXPERTS)]
    rem = [r & (BF16_ROWS - 1) for r in runs]
    base = [r - m for r, m in zip(runs, rem)]
    width = [m + n for m, n in zip(rem, counts)]
    win16 = [((w + (BF16_ROWS - 1)) >> BF16_ROWS_LOG2) << BF16_ROWS_LOG2 for w in width]
    nfull = [(w >> BF16_ROWS_LOG2) << BF16_ROWS_LOG2 for w in width]
    left = [w - f for w, f in zip(width, nfull)]
    offs, total = [], jnp.int32(0)
    for e in range(N_EXPERTS):
        offs.append(total)
        total = total + win16[e]

    sub = lax.broadcasted_iota(I32, (N_EXPERTS, 1), 0)

    def expert_column(vals):
        col = jnp.zeros((N_EXPERTS, 1), I32)
        for e in range(N_EXPERTS):
            col = jnp.where(sub == e, vals[e], col)
        return col

    off_col, win_col, base_col = expert_column(offs), expert_column(win16), expert_column(base)
    first_col = expert_column([o + m for o, m in zip(offs, rem)])
    slot_all = first_col.astype(F32) + rank
    slot1 = jnp.sum(jnp.where(sel1, slot_all, 0.0), axis=0, keepdims=True)
    slot2 = jnp.sum(jnp.where(sel2, slot_all, 0.0), axis=0, keepdims=True)

    rowi = lax.broadcasted_iota(I32, (slots, tb), 0)
    perm = jnp.where((rowi == slot1.astype(I32)) | (rowi == slot2.astype(I32)), 1.0, 0.0).astype(BF16)
    xs_sorted = _dot(perm, h_hi).astype(BF16)

    def copy(e, buf_row, hbm_row, size):
        return pltpu.make_async_copy(xs_buf.at[pl.ds(buf_row, size), :], xs_hbm.at[e, pl.ds(hbm_row, size), :], sem)

    def start(e):
        def fn(buf_row, hbm_row, size, cond):
            @pl.when(cond)
            def _():
                copy(e, buf_row, hbm_row, size).start()
        return fn

    def wait(e):
        def fn(buf_row, hbm_row, size, cond):
            @pl.when(cond)
            def _():
                copy(e, buf_row, hbm_row, size).wait()
        return fn

    for e in range(N_EXPERTS):
        _segment_copies(wait(e), prev_ref[e], prev_ref[2 * N_EXPERTS + e], prev_ref[N_EXPERTS + e])
    xs_buf[...] = xs_sorted
    for e in range(N_EXPERTS):
        head = pl.ds(pl.multiple_of(offs[e], BF16_ROWS), BF16_ROWS)
        xs_buf[head, :] = xs_buf[head, :] + carry_buf[e]
    for e in range(N_EXPERTS):
        _segment_copies(start(e), offs[e], base[e], nfull[e])
    for e in range(N_EXPERTS):
        tail = xs_buf[pl.ds(pl.multiple_of(offs[e] + nfull[e], BF16_ROWS), BF16_ROWS), :]
        carry_buf[e] = jnp.where(left[e] > 0, tail, jnp.zeros_like(tail))

    @pl.when(blk == nblk - 1)
    def _():
        for e in range(N_EXPERTS):
            _segment_copies(wait(e), offs[e], base[e], nfull[e])
        for e in range(N_EXPERTS):
            @pl.when(left[e] > 0)
            def _(e=e):
                flush = pltpu.make_async_copy(
                    carry_buf.at[e],
                    xs_hbm.at[e, pl.ds(pl.multiple_of(base[e] + nfull[e], BF16_ROWS), BF16_ROWS), :], sem)
                flush.start()
                flush.wait()

    route_ref[0] = jnp.concatenate([slot1, slot2, w_first, w_second, jnp.zeros((4, tb), F32)], axis=0)
    lane = lax.broadcasted_iota(I32, (N_EXPERTS, LANES), 1)
    after_col = expert_column([r + n for r, n in zip(runs, counts)])
    seg_ref[0] = jnp.where(lane == 0, off_col, jnp.where(lane == 1, win_col, jnp.where(
        lane == 2, base_col, jnp.where(lane == 3, after_col, 0))))
    for e in range(N_EXPERTS):
        run_ref[e] = runs[e] + counts[e]
        prev_ref[e] = offs[e]
        prev_ref[N_EXPERTS + e] = nfull[e]
        prev_ref[2 * N_EXPERTS + e] = base[e]


def _route_call(layer, x, mod, ng, wr2, cap):
    bsz, seq, d = x.shape
    tb = MOE_TB
    nsb = seq // tb
    nblk = bsz * nsb
    ri = lax.broadcasted_iota(I32, (tb, tb), 0)
    ci = lax.broadcasted_iota(I32, (tb, tb), 1)
    sut = (ri < ci).astype(BF16)
    return pl.pallas_call(
        functools.partial(_route_kernel, nsb, nblk),
        grid=(bsz, nsb),
        in_specs=[pl.BlockSpec((1, tb, d), lambda b, s: (b, s, 0)), _mod_spec(layer, d),
                  _const_spec(ng.shape), _const_spec(wr2.shape), _const_spec((tb, tb))],
        out_specs=[pl.BlockSpec(memory_space=pl.ANY),
                   pl.BlockSpec((1, 8, tb), lambda b, s: (b * nsb + s, 0, 0)),
                   pl.BlockSpec((1, N_EXPERTS, LANES), lambda b, s: (b * nsb + s, 0, 0))],
        out_shape=[jax.ShapeDtypeStruct((N_EXPERTS, cap, d), BF16),
                   jax.ShapeDtypeStruct((nblk, 8, tb), F32),
                   jax.ShapeDtypeStruct((nblk, N_EXPERTS, LANES), I32)],
        scratch_shapes=[pltpu.VMEM((MOE_SLOTS, d), BF16), pltpu.VMEM((N_EXPERTS, BF16_ROWS, d), BF16),
                        pltpu.SMEM((N_EXPERTS,), I32), pltpu.SMEM((3 * N_EXPERTS,), I32), pltpu.SemaphoreType.DMA],
        compiler_params=pltpu.CompilerParams(dimension_semantics=("arbitrary", "arbitrary"),
                                             vmem_limit_bytes=VMEM_LIMIT),
        name="moe_route",
    )(x, mod, ng, wr2, sut)


def _gmm_kernel(te_ref, trb_ref, tval_ref, tnv_ref, xs_ref, w1_ref, w3_ref, w2_ref, ys_ref):
    i = pl.program_id(0)

    @pl.when(tval_ref[i] == 1)
    def _():
        tm = xs_ref.shape[0]
        rows = lax.broadcasted_iota(I32, (tm, 1), 0)
        xs = xs_ref[...]
        xs = jnp.where(rows < tnv_ref[i], xs, jnp.zeros_like(xs))
        acc = None
        for j in range(w1_ref.shape[-1] // MOE_FF_CHUNK):
            cols = slice(j * MOE_FF_CHUNK, (j + 1) * MOE_FF_CHUNK)
            h1 = _dot(xs, w1_ref[:, cols])
            h3 = _dot(xs, w3_ref[:, cols])
            part = _dot((_silu(h1) * h3).astype(BF16), w2_ref[cols, :])
            acc = part if acc is None else acc + part
        ys_ref[...] = acc.astype(BF16)


def _gmm_call(idx, xs, w1, w3, w2, te, trb, tval, tnv):
    ne, cap, d = xs.shape
    tm = MOE_TILE_ROWS
    nt = te.shape[0]

    def wspec(w):
        return pl.BlockSpec((None, None) + w.shape[2:], lambda i, te, trb, tval, tnv: (idx, te[i], 0, 0),
                            pipeline_mode=pl.Buffered(1))

    return pl.pallas_call(
        _gmm_kernel,
        grid_spec=pltpu.PrefetchScalarGridSpec(
            num_scalar_prefetch=4,
            grid=(nt,),
            in_specs=[pl.BlockSpec((None, tm, d), lambda i, te, trb, tval, tnv: (te[i], trb[i], 0)),
                      wspec(w1), wspec(w3), wspec(w2)],
            out_specs=pl.BlockSpec((None, tm, d), lambda i, te, trb, tval, tnv: (te[i], trb[i], 0))),
        out_shape=jax.ShapeDtypeStruct((ne, cap, d), BF16),
        compiler_params=pltpu.CompilerParams(dimension_semantics=("arbitrary",), vmem_limit_bytes=VMEM_LIMIT),
        name="moe_experts",
    )(te, trb, tval, tnv, xs, w1, w3, w2)


def _combine_kernel(final, nsb, nblk, seg_ref, x_ref, mod_ref, route_ref, fng_ref, ys_hbm, o_ref, ys_buf, sem):
    tb, slots = MOE_TB, MOE_SLOTS
    blk = pl.program_id(0) * nsb + pl.program_id(1)
    cur = blk & 1

    def segments(b):
        base = b * (3 * N_EXPERTS)
        return ([seg_ref[base + e] for e in range(N_EXPERTS)],
                [seg_ref[base + N_EXPERTS + e] for e in range(N_EXPERTS)],
                [seg_ref[base + 2 * N_EXPERTS + e] for e in range(N_EXPERTS)])

    def fetch(b, buf, also, wait):
        offs, cnt16, runs = segments(b)
        for e in range(N_EXPERTS):
            def fn(buf_row, hbm_row, size, cond, e=e):
                @pl.when(cond & also)
                def _():
                    cp = pltpu.make_async_copy(ys_hbm.at[e, pl.ds(hbm_row, size), :],
                                               ys_buf.at[buf, pl.ds(buf_row, size), :], sem.at[buf])
                    cp.wait() if wait else cp.start()
            _segment_copies(fn, offs[e], runs[e], cnt16[e])
        return offs[N_EXPERTS - 1] + cnt16[N_EXPERTS - 1]

    @pl.when(blk == 0)
    def _():
        fetch(blk, cur, True, wait=False)

    fetch(jnp.minimum(blk + 1, nblk - 1), 1 - cur, blk + 1 < nblk, wait=False)
    total = fetch(blk, cur, True, wait=True)

    rowi = lax.broadcasted_iota(I32, (slots, 1), 0)
    ys = ys_buf[cur]
    ys = jnp.where(rowi < total, ys, jnp.zeros_like(ys))

    route = jnp.concatenate([route_ref[0], jnp.zeros((LANES - 8, tb), F32)], axis=0).T
    slot1, slot2 = route[:, 0:1].astype(I32), route[:, 1:2].astype(I32)
    lane = lax.broadcasted_iota(I32, (tb, slots), 1)
    comb = (jnp.where(lane == slot1, route[:, 2:3], 0.0) + jnp.where(lane == slot2, route[:, 3:4], 0.0)).astype(BF16)
    out = x_ref[0] + mod_ref[...][5:6] * _dot(comb, ys)
    o_ref[0] = _final_norm(out, fng_ref[...]) if final else out


def _combine_call(final, layer, x, mod, route, seg_flat, ys, fng):
    bsz, seq, d = x.shape
    tb = MOE_TB
    nsb = seq // tb
    return pl.pallas_call(
        functools.partial(_combine_kernel, final, nsb, bsz * nsb),
        grid_spec=pltpu.PrefetchScalarGridSpec(
            num_scalar_prefetch=1,
            grid=(bsz, nsb),
            in_specs=[pl.BlockSpec((1, tb, d), lambda b, s, seg: (b, s, 0)), _mod_spec(layer, d),
                      pl.BlockSpec((1, 8, tb), lambda b, s, seg: (b * nsb + s, 0, 0)),
                      _const_spec(fng.shape),
                      pl.BlockSpec(memory_space=pl.ANY)],
            out_specs=pl.BlockSpec((1, tb, d), lambda b, s, seg: (b, s, 0)),
            scratch_shapes=[pltpu.VMEM((2, MOE_SLOTS, d), BF16), pltpu.SemaphoreType.DMA((2,))]),
        out_shape=jax.ShapeDtypeStruct(x.shape, F32),
        input_output_aliases={1: 0},
        compiler_params=pltpu.CompilerParams(dimension_semantics=("arbitrary", "arbitrary"),
                                             vmem_limit_bytes=VMEM_LIMIT),
        name="moe_combine",
    )(seg_flat, x, mod, route, fng, ys)


def _moe_tiles(seg, tm, nt):
    total = seg[-1, :, 3]
    tiles = (total + tm - 1) // tm
    ends = jnp.cumsum(tiles)
    starts = ends - tiles
    ntiles = ends[-1]
    i = jnp.arange(nt, dtype=I32)
    valid = i < ntiles
    ie = jnp.minimum(i, ntiles - 1)
    te = jnp.minimum(jnp.sum((ie[:, None] >= ends[None, :]).astype(I32), axis=1), N_EXPERTS - 1)
    trb = ie - starts[te]
    tnv = jnp.clip(total[te] - trb * tm, 0, tm)
    return te.astype(I32), trb.astype(I32), valid.astype(I32), tnv.astype(I32)


def _moe_layer(final, layer, x, mod, ng, wr2, w1, w3, w2, fng):
    bsz, seq, d = x.shape
    tokens = bsz * seq
    nblk = tokens // MOE_TB
    tm = MOE_TILE_ROWS
    cap = -(-(tokens + BF16_ROWS) // tm) * tm
    nt = -(-2 * tokens // tm) + N_EXPERTS
    xs, route, seg = _route_call(layer, x, mod, ng, wr2, cap)
    te, trb, tval, tnv = _moe_tiles(seg, tm, nt)
    ys = _gmm_call(layer // 2, xs, w1, w3, w2, te, trb, tval, tnv)
    seg_flat = jnp.transpose(seg[:, :, 0:3], (0, 2, 1)).reshape(-1)
    return _combine_call(final, layer, x, mod, route, seg_flat, ys, fng)


def kernel(x, c, norm_mix_g, norm_ffn_g, final_norm_g, w_ada, b_ada, w_in, w_out, conv_w, sgu_norm_g, sgu_w, sgu_b,
           gla_w_gate, gla_b_gate, gla_norm_g, hgrn_lower_bounds, hgrn_norm_g, ffn_w1, ffn_w3, ffn_w2, moe_router,
           moe_w1, moe_w3, moe_w2):
    depth = w_in.shape[0]
    bsz, seq, d = x.shape
    assert d == D_MODEL and seq % MOE_TB == 0 and hgrn_lower_bounds.shape[0] == depth

    mod = _ada_call(c, w_ada, b_ada).reshape(depth, bsz, N_MOD, d)

    win = jnp.concatenate([w_in[:, :, :IN_PROJ_SPLIT],
                           jnp.zeros((depth, d, LANES - GLA_GATE_RANK), w_in.dtype),
                           w_in[:, :, IN_PROJ_SPLIT:]], axis=-1).astype(BF16)
    wout = w_out.astype(BF16)
    wgate = jnp.concatenate([gla_w_gate, jnp.zeros((depth, LANES - GLA_GATE_RANK, gla_w_gate.shape[-1]), F32)],
                            axis=1).astype(BF16)
    fng = final_norm_g.reshape(1, d)
    ffn_w = [w.astype(BF16) for w in (ffn_w1, ffn_w3, ffn_w2)]
    moe_w = [w.astype(BF16) for w in (moe_w1, moe_w3, moe_w2)]
    wr = jnp.concatenate([moe_router, jnp.zeros(moe_router.shape[:2] + (LANES - N_EXPERTS,), F32)], axis=-1)
    wr_hi = wr.astype(BF16)
    wr2 = jnp.concatenate([wr_hi, (wr - wr_hi.astype(F32)).astype(BF16)], axis=-1)

    for layer in range(depth):
        x = _mixer_call(
            layer, x, mod, norm_mix_g[layer].reshape(1, d), win, wout, conv_w[layer],
            sgu_norm_g[layer].reshape(1, GROUP_WIDTH),
            jnp.transpose(sgu_w[layer], (1, 0, 2)).reshape(SUPER, HEADS * SUPER),
            jnp.repeat(sgu_b[layer].T, HEAD_DIM, axis=1), wgate[layer], gla_b_gate[layer].reshape(1, -1),
            jnp.tile(gla_norm_g[layer], HEADS).reshape(1, GROUP_WIDTH), hgrn_lower_bounds,
            jnp.tile(hgrn_norm_g[layer], HEADS).reshape(1, GROUP_WIDTH))
        final = layer == depth - 1
        ng = norm_ffn_g[layer].reshape(1, d)
        if layer % 2 == 0:
            x = _ffn_call(final, layer, x, mod, ng, *ffn_w, fng)
        else:
            x = _moe_layer(final, layer, x, mod, ng, wr2[layer // 2], *moe_w, fng)
    return x
```

```python
import functools

import jax
import jax.numpy as jnp
from jax import lax
from jax.experimental import pallas as pl
from jax.experimental.pallas import tpu as pltpu

F32 = jnp.float32
BF16 = jnp.bfloat16
I32 = jnp.int32
EPS = 1e-6

D_MODEL = 1024
N_MOD = 6
GROUP_WIDTH = 256
HEADS = 4
HEAD_DIM = 64
GLA_KEY_DIM = 32
GLA_GATE_RANK = 16
GLA_GATE_NORMALIZER = 16.0
LA_CHUNK_LOG2 = 4
SUPER = 128
SUPER_LOG2 = 7
N_EXPERTS = 8
LANES = 128
BF16_ROWS = 16
BF16_ROWS_LOG2 = 4

C_CB, C_CC, C_CX, C_SU, C_SV = 0, 256, 512, 768, 1024
C_AQ, C_AK, C_AV, C_AG, C_AR = 1280, 1408, 1536, 1792, 1920
C_HQ, C_HF, C_HI, C_HG = 2176, 2432, 2688, 2944
IN_PROJ_PAD = 3200
IN_PROJ_SPLIT = 1808

VMEM_LIMIT = 56 * 1024 * 1024

MIXER_ROWS = 512
FFN_ROWS = 512
MOE_TB = 512
MOE_SLOTS = 2 * MOE_TB + 2 * LANES
MOE_TILE_ROWS = 512
MOE_FF_CHUNK = 512


def _silu(x):
    return x * jax.nn.sigmoid(x)


def _dot(a, b):
    return jnp.dot(a, b, preferred_element_type=F32)


def _dot_nt(a, b):
    return lax.dot_general(a, b, (((1,), (1,)), ((), ())), preferred_element_type=F32)


def _dot_tn(a, b):
    return lax.dot_general(a, b, (((0,), (0,)), ((), ())), preferred_element_type=F32)


def _split_bf16(x, pieces):
    out, r = [], x
    for _ in range(pieces):
        t = r.astype(BF16)
        out.append(t)
        r = r - t.astype(F32)
    return out


def _mod_norm(xt, g, shift, scale):
    ms = jnp.mean(xt * xt, axis=-1, keepdims=True)
    return (xt * lax.rsqrt(ms + EPS)) * g * (1.0 + scale) + shift


def _const_spec(shape):
    zeros = (0,) * len(shape)
    return pl.BlockSpec(shape, lambda *_: zeros, pipeline_mode=pl.Buffered(1))


def _layer_spec(arr, layer):
    zeros = (0,) * (arr.ndim - 1)
    return pl.BlockSpec((None,) + arr.shape[1:], lambda *_: (layer,) + zeros, pipeline_mode=pl.Buffered(1))


def _mod_spec(layer, d):
    return pl.BlockSpec((None, None, N_MOD, d), lambda b, s, *_: (layer, b, 0, 0))


def _ada_kernel(c_ref, w_ref, b_ref, o_ref):
    cond = _silu(c_ref[...])
    o_ref[0] = _dot(cond.astype(BF16), w_ref[0].astype(BF16)) + b_ref[0]


def _ada_call(c, w_ada, b_ada):
    depth, d, n = w_ada.shape
    bsz = c.shape[0]
    tn = n // 4
    return pl.pallas_call(
        _ada_kernel,
        grid=(depth, n // tn),
        in_specs=[pl.BlockSpec((bsz, d), lambda l, j: (0, 0)),
                  pl.BlockSpec((1, d, tn), lambda l, j: (l, 0, j)),
                  pl.BlockSpec((1, 1, tn), lambda l, j: (l, 0, j))],
        out_specs=pl.BlockSpec((1, bsz, tn), lambda l, j: (l, 0, j)),
        out_shape=jax.ShapeDtypeStruct((depth, bsz, n), F32),
        compiler_params=pltpu.CompilerParams(dimension_semantics=("arbitrary", "arbitrary"),
                                             vmem_limit_bytes=VMEM_LIMIT),
        name="adaln_mod",
    )(c, w_ada, b_ada.reshape(depth, 1, n))


def _pair_masks():
    t = lax.broadcasted_iota(I32, (SUPER, HEADS * SUPER), 0)
    s = lax.broadcasted_iota(I32, (SUPER, HEADS * SUPER), 1) & (SUPER - 1)
    masks = [((t >> LA_CHUNK_LOG2) == (s >> LA_CHUNK_LOG2)) & (s <= t)]
    for sh in range(LA_CHUNK_LOG2 + 1, SUPER_LOG2 + 1):
        half = 1 << (sh - 1)
        masks.append(((t >> sh) == (s >> sh)) & ((t & half) != 0) & ((s & half) == 0))
    return masks


def _level_refs(c):
    kt = c.shape[-1]
    n16 = SUPER >> LA_CHUNK_LOG2
    mid = (1 << LA_CHUNK_LOG2) // 2
    refs = [jnp.broadcast_to(c.reshape(n16, 1 << LA_CHUNK_LOG2, kt)[:, mid:mid + 1, :],
                             (n16, 1 << LA_CHUNK_LOG2, kt)).reshape(SUPER, kt)]
    for sh in range(LA_CHUNK_LOG2 + 1, SUPER_LOG2 + 1):
        nb, bs, half = SUPER >> sh, 1 << sh, 1 << (sh - 1)
        refs.append(jnp.broadcast_to(c.reshape(nb, bs, kt)[:, half - 1:half, :], (nb, bs, kt)).reshape(SUPER, kt))
    return refs


def _head_rows(a):
    w = a.shape[-1] // HEADS
    ab = a.astype(BF16)
    lane_head = lax.broadcasted_iota(I32, a.shape, 1) // w
    return jnp.concatenate([jnp.where(lane_head == h, ab, jnp.zeros_like(ab)) for h in range(HEADS)], axis=0)


def _gated_linear_attention_step(q, k, v, log_a, st_ref, ltri, masks):
    kt = q.shape[-1]
    hk = kt // HEADS
    a_hi, a_lo = _split_bf16(log_a, 2)
    c = _dot(ltri, a_hi) + _dot(ltri, a_lo)

    def decay_factors(lvl, cref):
        d = c - cref
        if lvl == 0:
            return jnp.exp(d), jnp.exp(-d)
        e = jnp.exp(-jnp.abs(d))
        return e, e

    scores = None
    for lvl, (cref, mask) in enumerate(zip(_level_refs(c), masks)):
        eq, ek = decay_factors(lvl, cref)
        s_lvl = _dot_nt((q * eq).astype(BF16), _head_rows(k * ek))
        scores = jnp.where(mask, s_lvl, 0.0 if scores is None else scores)
    o = _dot(scores.astype(BF16), _head_rows(v))

    st = st_ref[...]
    o = o + _dot_nt((q * jnp.exp(c)).astype(BF16), st.astype(BF16))
    c_last = c[SUPER - 1:SUPER, :]
    kd = (k * jnp.exp(c_last - c)).astype(BF16)
    upd = _dot_tn(v.astype(BF16), kd)
    diag = (lax.broadcasted_iota(I32, (GROUP_WIDTH, kt), 0) // HEAD_DIM) == (
        lax.broadcasted_iota(I32, (GROUP_WIDTH, kt), 1) // hk)
    st_ref[...] = st * jnp.exp(c_last) + jnp.where(diag, upd, 0.0)
    return o


def _head_rms_gate(o, head_mean, g, gate):
    ms = _dot((o * o).astype(BF16), head_mean)
    return (o * lax.rsqrt(ms + EPS)) * g * _silu(gate)


def _mixer_kernel(layer, hr, x_ref, mod_ref, ng_ref, win_ref, wout_ref, convw_ref, sgug_ref, sguw_ref, sgub_ref,
                  wgate_ref, bgate_ref, glag_ref, hlb_ref, hgng_ref, o_ref,
                  p0_ref, p1_ref, y0_ref, y1_ref, zbuf, st_gla, st_hgrn):
    @pl.when(pl.program_id(1) == 0)
    def _():
        zbuf[:, 0:8, :] = jnp.zeros((2, 8, GROUP_WIDTH), F32)
        st_gla[...] = jnp.zeros_like(st_gla)
        st_hgrn[...] = jnp.zeros_like(st_hgrn)

    halves = ((p0_ref, y0_ref), (p1_ref, y1_ref))
    for hf, (p_ref, _) in enumerate(halves):
        xt = x_ref[hf]
        mod = mod_ref[hf]
        ms = jnp.mean(xt * xt, axis=-1, keepdims=True)
        h = (xt * lax.rsqrt(ms + EPS)) * (ng_ref[...] * (1.0 + mod[1:2])) + mod[0:1]
        p_ref[...] = _dot(h.astype(BF16), win_ref[...])

    masks = _pair_masks()
    ri = lax.broadcasted_iota(I32, (SUPER, SUPER), 0)
    ci = lax.broadcasted_iota(I32, (SUPER, SUPER), 1)
    ltri = jnp.where(ci <= ri, 1.0, 0.0).astype(BF16)
    t_ws = lax.broadcasted_iota(I32, (SUPER, HEADS * SUPER), 0)
    s_ws = lax.broadcasted_iota(I32, (SUPER, HEADS * SUPER), 1) & (SUPER - 1)
    ws = jnp.where(s_ws <= t_ws, sguw_ref[...], 0.0).astype(BF16)
    hrow = lax.broadcasted_iota(I32, (GROUP_WIDTH, GROUP_WIDTH), 0) // HEAD_DIM
    hcol = lax.broadcasted_iota(I32, (GROUP_WIDTH, GROUP_WIDTH), 1) // HEAD_DIM
    head_mean = jnp.where(hrow == hcol, 1.0 / HEAD_DIM, 0.0).astype(BF16)
    group_mean = jnp.full((GROUP_WIDTH, GROUP_WIDTH), 1.0 / GROUP_WIDTH, BF16)
    lb_all = hlb_ref[...]
    lb_e = jnp.exp(lb_all - jnp.max(lb_all, axis=0, keepdims=True))
    lb_sm = lb_e / jnp.sum(lb_e, axis=0, keepdims=True)
    lb = jnp.zeros((1, GROUP_WIDTH), F32)
    for j in range(1, layer + 1):
        lb = lb + lb_sm[j:j + 1]
    cw = convw_ref[...]

    for hf, (p_ref, y_ref) in enumerate(halves):
        z = p_ref[:,C_CC:C_CC + GROUP_WIDTH] * p_ref[:,C_CX:C_CX + GROUP_WIDTH]
        zbuf[hf, 8:8 + hr, :] = z
        conv = cw[0:1] * zbuf[hf, 6:6 + hr, :] + cw[1:2] * zbuf[hf, 7:7 + hr, :] + cw[2:3] * z
        y_ref[:, 0:GROUP_WIDTH] = (p_ref[:,C_CB:C_CB + GROUP_WIDTH] * conv).astype(BF16)
        zbuf[hf, 0:8, :] = zbuf[hf, hr:hr + 8, :]

        sv = p_ref[:,C_SV:C_SV + GROUP_WIDTH]
        sv_hi, sv_lo = _split_bf16(sv, 2)
        dv = sv - (_dot(sv_hi, group_mean) + _dot(sv_lo, group_mean))
        var = _dot((dv * dv).astype(BF16), group_mean)
        vn = (dv * lax.rsqrt(var + EPS)) * sgug_ref[...]
        gate_logit = _dot(p_ref[:,C_AG:C_AG + LANES].astype(BF16), wgate_ref[...]) + bgate_ref[...]
        log_a = (jnp.minimum(gate_logit, 0.0) - jnp.log(1.0 + jnp.exp(-jnp.abs(gate_logit)))) * (
            1.0 / GLA_GATE_NORMALIZER)
        f = lb + (1.0 - lb) * jax.nn.sigmoid(p_ref[:,C_HF:C_HF + GROUP_WIDTH])
        log_f = jnp.log(f)

        o_gla, o_hgrn = [], []
        for i in range(hr // SUPER):
            r = slice(i * SUPER, (i + 1) * SUPER)
            mixed = _dot(ws, _head_rows(vn[r])) + sgub_ref[...]
            y_ref[r, GROUP_WIDTH:2 * GROUP_WIDTH] = (p_ref[r,C_SU:C_SU + GROUP_WIDTH] * mixed).astype(BF16)
            o_gla.append(_gated_linear_attention_step(
                p_ref[r,C_AQ:C_AQ + LANES] * (GLA_KEY_DIM ** -0.5), p_ref[r,C_AK:C_AK + LANES],
                p_ref[r,C_AV:C_AV + GROUP_WIDTH], log_a[r], st_gla.at[hf], ltri, masks))
            o_hgrn.append(_gated_linear_attention_step(
                p_ref[r,C_HQ:C_HQ + GROUP_WIDTH], 1.0 - f[r], p_ref[r,C_HI:C_HI + GROUP_WIDTH], log_f[r],
                st_hgrn.at[hf], ltri, masks))
        y_ref[:, 2 * GROUP_WIDTH:3 * GROUP_WIDTH] = _head_rms_gate(
            jnp.concatenate(o_gla, axis=0), head_mean, glag_ref[...], p_ref[:,C_AR:C_AR + GROUP_WIDTH]).astype(BF16)
        y_ref[:, 3 * GROUP_WIDTH:4 * GROUP_WIDTH] = _head_rms_gate(
            jnp.concatenate(o_hgrn, axis=0), head_mean, hgng_ref[...], p_ref[:,C_HG:C_HG + GROUP_WIDTH]).astype(BF16)

        o_ref[hf] = x_ref[hf] + mod_ref[hf][2:3] * _dot(y_ref[...], wout_ref[...])


def _mixer_call(layer, x, mod, ng, win, wout, convw, sgug, sguw, sgub, wgate, bgate, glag, hlb, hgng):
    bsz, seq, d = x.shape
    hr = MIXER_ROWS
    assert bsz % 2 == 0 and seq % hr == 0
    small = [convw, sgug, sguw, sgub, wgate, bgate, glag, hlb, hgng]
    return pl.pallas_call(
        functools.partial(_mixer_kernel, layer, hr),
        grid=(bsz // 2, seq // hr),
        in_specs=[pl.BlockSpec((2, hr, d), lambda b, s: (b, s, 0)),
                  pl.BlockSpec((None, 2, N_MOD, d), lambda b, s: (layer, b, 0, 0)), _const_spec(ng.shape),
                  _layer_spec(win, layer), _layer_spec(wout, layer)] + [_const_spec(a.shape) for a in small],
        out_specs=pl.BlockSpec((2, hr, d), lambda b, s: (b, s, 0)),
        out_shape=jax.ShapeDtypeStruct(x.shape, F32),
        scratch_shapes=[pltpu.VMEM((hr, IN_PROJ_PAD), F32), pltpu.VMEM((hr, IN_PROJ_PAD), F32),
                        pltpu.VMEM((hr, d), BF16), pltpu.VMEM((hr, d), BF16),
                        pltpu.VMEM((2, hr + 8, GROUP_WIDTH), F32),
                        pltpu.VMEM((2, GROUP_WIDTH, HEADS * GLA_KEY_DIM), F32),
                        pltpu.VMEM((2, GROUP_WIDTH, GROUP_WIDTH), F32)],
        input_output_aliases={0: 0} if layer > 0 else {},
        compiler_params=pltpu.CompilerParams(dimension_semantics=("parallel", "arbitrary"),
                                             vmem_limit_bytes=VMEM_LIMIT),
        name=f"mixer_l{layer}",
    )(x, mod, ng, win, wout, *small)


def _final_norm(out, g):
    ms = jnp.mean(out * out, axis=-1, keepdims=True)
    return (out * lax.rsqrt(ms + EPS)) * g


def _ffn_kernel(final, x_ref, mod_ref, ng_ref, w1_ref, w3_ref, w2_ref, fng_ref, o_ref):
    xt = x_ref[0]
    mod = mod_ref[...]
    hb = _mod_norm(xt, ng_ref[...], mod[3:4], mod[4:5]).astype(BF16)
    h1 = _dot(hb, w1_ref[...])
    h3 = _dot(hb, w3_ref[...])
    a = (_silu(h1) * h3).astype(BF16)
    out = xt + mod[5:6] * _dot(a, w2_ref[...])
    o_ref[0] = _final_norm(out, fng_ref[...]) if final else out


def _ffn_call(final, layer, x, mod, ng, w1, w3, w2, fng):
    bsz, seq, d = x.shape
    tm = FFN_ROWS
    idx = layer // 2
    return pl.pallas_call(
        functools.partial(_ffn_kernel, final),
        grid=(bsz, seq // tm),
        in_specs=[pl.BlockSpec((1, tm, d), lambda b, s: (b, s, 0)), _mod_spec(layer, d), _const_spec(ng.shape),
                  _layer_spec(w1, idx), _layer_spec(w3, idx), _layer_spec(w2, idx), _const_spec(fng.shape)],
        out_specs=pl.BlockSpec((1, tm, d), lambda b, s: (b, s, 0)),
        out_shape=jax.ShapeDtypeStruct(x.shape, F32),
        input_output_aliases={0: 0},
        compiler_params=pltpu.CompilerParams(dimension_semantics=("parallel", "arbitrary"),
                                             vmem_limit_bytes=VMEM_LIMIT),
        name="dense_ffn",
    )(x, mod, ng, w1, w3, w2, fng)


SEG_SIZES = tuple(BF16_ROWS << i for i in range(5, -1, -1))


def _segment_copies(fn, buf_off, hbm_off, nrows):
    pos = jnp.int32(0)
    for size in SEG_SIZES:
        fn(pl.multiple_of(buf_off + pos, BF16_ROWS), pl.multiple_of(hbm_off + pos, BF16_ROWS), size,
           (nrows & size) != 0)
        pos = pos + (nrows & size)


def _route_kernel(nsb, nblk, x_ref, mod_ref, ng_ref, wr_ref, sut_ref, xs_hbm, route_ref, seg_ref,
                  xs_buf, carry_buf, run_ref, prev_ref, sem):
    tb, slots = MOE_TB, MOE_SLOTS
    blk = pl.program_id(0) * nsb + pl.program_id(1)

    @pl.when(blk == 0)
    def _():
        carry_buf[...] = jnp.zeros_like(carry_buf)
        for e in range(N_EXPERTS):
            run_ref[e] = 0
        for i in range(3 * N_EXPERTS):
            prev_ref[i] = 0

    mod = mod_ref[...]
    h = _mod_norm(x_ref[0], ng_ref[...], mod[3:4], mod[4:5])
    h_hi, h_lo = _split_bf16(h, 2)
    lg = _dot(h_hi, wr_ref[...]) + _dot(h_lo, wr_ref[...])
    logits = lg[:, 0:LANES] + lg[:, LANES:2 * LANES]
    lt = logits.T[0:N_EXPERTS, :]
    eid = lax.broadcasted_iota(I32, (N_EXPERTS, tb), 0)
    m1 = jnp.max(lt, axis=0, keepdims=True)
    i1 = jnp.min(jnp.where(lt == m1, eid, N_EXPERTS), axis=0, keepdims=True)
    lt2 = jnp.where(eid == i1, -jnp.inf, lt)
    m2 = jnp.max(lt2, axis=0, keepdims=True)
    i2 = jnp.min(jnp.where(lt2 == m2, eid, N_EXPERTS), axis=0, keepdims=True)
    e2 = jnp.exp(m2 - m1)
    w_first = 1.0 / (1.0 + e2)
    w_second = e2 / (1.0 + e2)

    sel1, sel2 = eid == i1, eid == i2
    onehot = jnp.where(sel1 | sel2, 1.0, 0.0).astype(F32)
    rank = _dot(onehot.astype(BF16), sut_ref[...])
    counts = [jnp.sum(onehot[e:e + 1, :]).astype(I32) for e in range(N_EXPERTS)]
    runs = [run_ref[e] for e in range(N_EXPERTS)]
    rem = [r & (BF16_ROWS - 1) for r in runs]
    base = [r - m for r, m in zip(runs, rem)]
    width = [m + n for m, n in zip(rem, counts)]
    win16 = [((w + (BF16_ROWS - 1)) >> BF16_ROWS_LOG2) << BF16_ROWS_LOG2 for w in width]
    nfull = [(w >> BF16_ROWS_LOG2) << BF16_ROWS_LOG2 for w in width]
    left = [w - f for w, f in zip(width, nfull)]
    offs, total = [], jnp.int32(0)
    for e in range(N_EXPERTS):
        offs.append(total)
        total = total + win16[e]

    sub = lax.broadcasted_iota(I32, (N_EXPERTS, 1), 0)

    def expert_column(vals):
        col = jnp.zeros((N_EXPERTS, 1), I32)
        for e in range(N_EXPERTS):
            col = jnp.where(sub == e, vals[e], col)
        return col

    off_col, win_col, base_col = expert_column(offs), expert_column(win16), expert_column(base)
    first_col = expert_column([o + m for o, m in zip(offs, rem)])
    slot_all = first_col.astype(F32) + rank
    slot1 = jnp.sum(jnp.where(sel1, slot_all, 0.0), axis=0, keepdims=True)
    slot2 = jnp.sum(jnp.where(sel2, slot_all, 0.0), axis=0, keepdims=True)

    rowi = lax.broadcasted_iota(I32, (slots, tb), 0)
    perm = jnp.where((rowi == slot1.astype(I32)) | (rowi == slot2.astype(I32)), 1.0, 0.0).astype(BF16)
    xs_sorted = _dot(perm, h_hi).astype(BF16)

    def copy(e, buf_row, hbm_row, size):
        return pltpu.make_async_copy(xs_buf.at[pl.ds(buf_row, size), :], xs_hbm.at[e, pl.ds(hbm_row, size), :], sem)

    def start(e):
        def fn(buf_row, hbm_row, size, cond):
            @pl.when(cond)
            def _():
                copy(e, buf_row, hbm_row, size).start()
        return fn

    def wait(e):
        def fn(buf_row, hbm_row, size, cond):
            @pl.when(cond)
            def _():
                copy(e, buf_row, hbm_row, size).wait()
        return fn

    for e in range(N_EXPERTS):
        _segment_copies(wait(e), prev_ref[e], prev_ref[2 * N_EXPERTS + e], prev_ref[N_EXPERTS + e])
    xs_buf[...] = xs_sorted
    for e in range(N_EXPERTS):
        head = pl.ds(pl.multiple_of(offs[e], BF16_ROWS), BF16_ROWS)
        xs_buf[head, :] = xs_buf[head, :] + carry_buf[e]
    for e in range(N_EXPERTS):
        _segment_copies(start(e), offs[e], base[e], nfull[e])
    for e in range(N_EXPERTS):
        tail = xs_buf[pl.ds(pl.multiple_of(offs[e] + nfull[e], BF16_ROWS), BF16_ROWS), :]
        carry_buf[e] = jnp.where(left[e] > 0, tail, jnp.zeros_like(tail))

    @pl.when(blk == nblk - 1)
    def _():
        for e in range(N_EXPERTS):
            _segment_copies(wait(e), offs[e], base[e], nfull[e])
        for e in range(N_EXPERTS):
            @pl.when(left[e] > 0)
            def _(e=e):
                flush = pltpu.make_async_copy(
                    carry_buf.at[e],
                    xs_hbm.at[e, pl.ds(pl.multiple_of(base[e] + nfull[e], BF16_ROWS), BF16_ROWS), :], sem)
                flush.start()
                flush.wait()

    route_ref[0] = jnp.concatenate([slot1, slot2, w_first, w_second, jnp.zeros((4, tb), F32)], axis=0)
    lane = lax.broadcasted_iota(I32, (N_EXPERTS, LANES), 1)
    after_col = expert_column([r + n for r, n in zip(runs, counts)])
    seg_ref[0] = jnp.where(lane == 0, off_col, jnp.where(lane == 1, win_col, jnp.where(
        lane == 2, base_col, jnp.where(lane == 3, after_col, 0))))
    for e in range(N_EXPERTS):
        run_ref[e] = runs[e] + counts[e]
        prev_ref[e] = offs[e]
        prev_ref[N_EXPERTS + e] = nfull[e]
        prev_ref[2 * N_EXPERTS + e] = base[e]


def _route_call(layer, x, mod, ng, wr2, cap):
    bsz, seq, d = x.shape
    tb = MOE_TB
    nsb = seq // tb
    nblk = bsz * nsb
    ri = lax.broadcasted_iota(I32, (tb, tb), 0)
    ci = lax.broadcasted_iota(I32, (tb, tb), 1)
    sut = (ri < ci).astype(BF16)
    return pl.pallas_call(
        functools.partial(_route_kernel, nsb, nblk),
        grid=(bsz, nsb),
        in_specs=[pl.BlockSpec((1, tb, d), lambda b, s: (b, s, 0)), _mod_spec(layer, d),
                  _const_spec(ng.shape), _const_spec(wr2.shape), _const_spec((tb, tb))],
        out_specs=[pl.BlockSpec(memory_space=pl.ANY),
                   pl.BlockSpec((1, 8, tb), lambda b, s: (b * nsb + s, 0, 0)),
                   pl.BlockSpec((1, N_EXPERTS, LANES), lambda b, s: (b * nsb + s, 0, 0))],
        out_shape=[jax.ShapeDtypeStruct((N_EXPERTS, cap, d), BF16),
                   jax.ShapeDtypeStruct((nblk, 8, tb), F32),
                   jax.ShapeDtypeStruct((nblk, N_EXPERTS, LANES), I32)],
        scratch_shapes=[pltpu.VMEM((MOE_SLOTS, d), BF16), pltpu.VMEM((N_EXPERTS, BF16_ROWS, d), BF16),
                        pltpu.SMEM((N_EXPERTS,), I32), pltpu.SMEM((3 * N_EXPERTS,), I32), pltpu.SemaphoreType.DMA],
        compiler_params=pltpu.CompilerParams(dimension_semantics=("arbitrary", "arbitrary"),
                                             vmem_limit_bytes=VMEM_LIMIT),
        name="moe_route",
    )(x, mod, ng, wr2, sut)


def _gmm_kernel(idx, te_ref, trb_ref, tval_ref, tnv_ref, xs_ref, w1_hbm, w3_hbm, w2_hbm, ys_ref,
                w1_ref, w3_ref, w2_ref, stage_in, stage_out, sem):
    i = pl.program_id(0)
    e = te_ref[i]
    tf = MOE_FF_CHUNK
    nf = w1_ref.shape[-1] // tf

    @pl.when((i == 0) | (e != te_ref[jnp.maximum(i - 1, 0)]))
    def _():
        def chunk(k, slot):
            which, j = divmod(k, nf)
            if which < 2:
                src = (w1_hbm, w3_hbm)[which].at[idx, e, :, pl.ds(j * tf, tf)]
                return pltpu.make_async_copy(src, stage_in.at[slot], sem.at[slot])
            return pltpu.make_async_copy(w2_hbm.at[idx, e, pl.ds(j * tf, tf), :], stage_out.at[slot], sem.at[slot])

        chunk(0, 0).start()
        for k in range(3 * nf):
            slot = k % 2
            if k + 1 < 3 * nf:
                chunk(k + 1, 1 - slot).start()
            chunk(k, slot).wait()
            which, j = divmod(k, nf)
            if which < 2:
                (w1_ref, w3_ref)[which][:, j * tf:(j + 1) * tf] = stage_in[slot].astype(BF16)
            else:
                w2_ref[j * tf:(j + 1) * tf, :] = stage_out[slot].astype(BF16)

    @pl.when(tval_ref[i] == 1)
    def _():
        tm = xs_ref.shape[0]
        rows = lax.broadcasted_iota(I32, (tm, 1), 0)
        xs = xs_ref[...]
        xs = jnp.where(rows < tnv_ref[i], xs, jnp.zeros_like(xs))
        acc = None
        for j in range(w1_ref.shape[-1] // MOE_FF_CHUNK):
            cols = slice(j * MOE_FF_CHUNK, (j + 1) * MOE_FF_CHUNK)
            h1 = _dot(xs, w1_ref[:, cols])
            h3 = _dot(xs, w3_ref[:, cols])
            part = _dot((_silu(h1) * h3).astype(BF16), w2_ref[cols, :])
            acc = part if acc is None else acc + part
        ys_ref[...] = acc.astype(BF16)


def _gmm_call(idx, xs, w1, w3, w2, te, trb, tval, tnv):
    ne, cap, d = xs.shape
    dff = w1.shape[-1]
    tm = MOE_TILE_ROWS
    nt = te.shape[0]
    anywhere = pl.BlockSpec(memory_space=pl.ANY)

    return pl.pallas_call(
        functools.partial(_gmm_kernel, idx),
        grid_spec=pltpu.PrefetchScalarGridSpec(
            num_scalar_prefetch=4,
            grid=(nt,),
            in_specs=[pl.BlockSpec((None, tm, d), lambda i, te, trb, tval, tnv: (te[i], trb[i], 0)),
                      anywhere, anywhere, anywhere],
            out_specs=pl.BlockSpec((None, tm, d), lambda i, te, trb, tval, tnv: (te[i], trb[i], 0)),
            scratch_shapes=[pltpu.VMEM((d, dff), BF16), pltpu.VMEM((d, dff), BF16), pltpu.VMEM((dff, d), BF16),
                            pltpu.VMEM((2, d, MOE_FF_CHUNK), F32), pltpu.VMEM((2, MOE_FF_CHUNK, d), F32),
                            pltpu.SemaphoreType.DMA((2,))]),
        out_shape=jax.ShapeDtypeStruct((ne, cap, d), BF16),
        compiler_params=pltpu.CompilerParams(dimension_semantics=("arbitrary",), vmem_limit_bytes=VMEM_LIMIT),
        name="moe_experts",
    )(te, trb, tval, tnv, xs, w1, w3, w2)


def _combine_kernel(final, nsb, nblk, seg_ref, x_ref, mod_ref, route_ref, fng_ref, ys_hbm, o_ref, ys_buf, sem):
    tb, slots = MOE_TB, MOE_SLOTS
    blk = pl.program_id(0) * nsb + pl.program_id(1)
    cur = blk & 1

    def segments(b):
        base = b * (3 * N_EXPERTS)
        return ([seg_ref[base + e] for e in range(N_EXPERTS)],
                [seg_ref[base + N_EXPERTS + e] for e in range(N_EXPERTS)],
                [seg_ref[base + 2 * N_EXPERTS + e] for e in range(N_EXPERTS)])

    def fetch(b, buf, also, wait):
        offs, cnt16, runs = segments(b)
        for e in range(N_EXPERTS):
            def fn(buf_row, hbm_row, size, cond, e=e):
                @pl.when(cond & also)
                def _():
                    cp = pltpu.make_async_copy(ys_hbm.at[e, pl.ds(hbm_row, size), :],
                                               ys_buf.at[buf, pl.ds(buf_row, size), :], sem.at[buf])
                    cp.wait() if wait else cp.start()
            _segment_copies(fn, offs[e], runs[e], cnt16[e])
        return offs[N_EXPERTS - 1] + cnt16[N_EXPERTS - 1]

    @pl.when(blk == 0)
    def _():
        fetch(blk, cur, True, wait=False)

    fetch(jnp.minimum(blk + 1, nblk - 1), 1 - cur, blk + 1 < nblk, wait=False)
    total = fetch(blk, cur, True, wait=True)

    rowi = lax.broadcasted_iota(I32, (slots, 1), 0)
    ys = ys_buf[cur]
    ys = jnp.where(rowi < total, ys, jnp.zeros_like(ys))

    route = jnp.concatenate([route_ref[0], jnp.zeros((LANES - 8, tb), F32)], axis=0).T
    slot1, slot2 = route[:, 0:1].astype(I32), route[:, 1:2].astype(I32)
    lane = lax.broadcasted_iota(I32, (tb, slots), 1)
    comb = (jnp.where(lane == slot1, route[:, 2:3], 0.0) + jnp.where(lane == slot2, route[:, 3:4], 0.0)).astype(BF16)
    out = x_ref[0] + mod_ref[...][5:6] * _dot(comb, ys)
    o_ref[0] = _final_norm(out, fng_ref[...]) if final else out


def _combine_call(final, layer, x, mod, route, seg_flat, ys, fng):
    bsz, seq, d = x.shape
    tb = MOE_TB
    nsb = seq // tb
    return pl.pallas_call(
        functools.partial(_combine_kernel, final, nsb, bsz * nsb),
        grid_spec=pltpu.PrefetchScalarGridSpec(
            num_scalar_prefetch=1,
            grid=(bsz, nsb),
            in_specs=[pl.BlockSpec((1, tb, d), lambda b, s, seg: (b, s, 0)), _mod_spec(layer, d),
                      pl.BlockSpec((1, 8, tb), lambda b, s, seg: (b * nsb + s, 0, 0)),
                      _const_spec(fng.shape),
                      pl.BlockSpec(memory_space=pl.ANY)],
            out_specs=pl.BlockSpec((1, tb, d), lambda b, s, seg: (b, s, 0)),
            scratch_shapes=[pltpu.VMEM((2, MOE_SLOTS, d), BF16), pltpu.SemaphoreType.DMA((2,))]),
        out_shape=jax.ShapeDtypeStruct(x.shape, F32),
        input_output_aliases={1: 0},
        compiler_params=pltpu.CompilerParams(dimension_semantics=("arbitrary", "arbitrary"),
                                             vmem_limit_bytes=VMEM_LIMIT),
        name="moe_combine",
    )(seg_flat, x, mod, route, fng, ys)


def _moe_tiles(seg, tm, nt):
    total = seg[-1, :, 3]
    tiles = (total + tm - 1) // tm
    ends = jnp.cumsum(tiles)
    starts = ends - tiles
    ntiles = ends[-1]
    i = jnp.arange(nt, dtype=I32)
    valid = i < ntiles
    ie = jnp.minimum(i, ntiles - 1)
    te = jnp.minimum(jnp.sum((ie[:, None] >= ends[None, :]).astype(I32), axis=1), N_EXPERTS - 1)
    trb = ie - starts[te]
    tnv = jnp.clip(total[te] - trb * tm, 0, tm)
    return te.astype(I32), trb.astype(I32), valid.astype(I32), tnv.astype(I32)


def _moe_layer(final, layer, x, mod, ng, wr2, w1, w3, w2, fng):
    bsz, seq, d = x.shape
    tokens = bsz * seq
    nblk = tokens // MOE_TB
    tm = MOE_TILE_ROWS
    cap = -(-(tokens + BF16_ROWS) // tm) * tm
    nt = -(-2 * tokens // tm) + N_EXPERTS
    xs, route, seg = _route_call(layer, x, mod, ng, wr2, cap)
    te, trb, tval, tnv = _moe_tiles(seg, tm, nt)
    ys = _gmm_call(layer // 2, xs, w1, w3, w2, te, trb, tval, tnv)
    seg_flat = jnp.transpose(seg[:, :, 0:3], (0, 2, 1)).reshape(-1)
    return _combine_call(final, layer, x, mod, route, seg_flat, ys, fng)


def kernel(x, c, norm_mix_g, norm_ffn_g, final_norm_g, w_ada, b_ada, w_in, w_out, conv_w, sgu_norm_g, sgu_w, sgu_b,
           gla_w_gate, gla_b_gate, gla_norm_g, hgrn_lower_bounds, hgrn_norm_g, ffn_w1, ffn_w3, ffn_w2, moe_router,
           moe_w1, moe_w3, moe_w2):
    depth = w_in.shape[0]
    bsz, seq, d = x.shape
    assert d == D_MODEL and seq % MOE_TB == 0 and hgrn_lower_bounds.shape[0] == depth

    mod = _ada_call(c, w_ada, b_ada).reshape(depth, bsz, N_MOD, d)

    win = jnp.concatenate([w_in[:, :, :IN_PROJ_SPLIT],
                           jnp.zeros((depth, d, LANES - GLA_GATE_RANK), w_in.dtype),
                           w_in[:, :, IN_PROJ_SPLIT:]], axis=-1).astype(BF16)
    wout = w_out.astype(BF16)
    wgate = jnp.concatenate([gla_w_gate, jnp.zeros((depth, LANES - GLA_GATE_RANK, gla_w_gate.shape[-1]), F32)],
                            axis=1).astype(BF16)
    fng = final_norm_g.reshape(1, d)
    ffn_w = [w.astype(BF16) for w in (ffn_w1, ffn_w3, ffn_w2)]
    moe_w = (moe_w1, moe_w3, moe_w2)
    wr = jnp.concatenate([moe_router, jnp.zeros(moe_router.shape[:2] + (LANES - N_EXPERTS,), F32)], axis=-1)
    wr_hi = wr.astype(BF16)
    wr2 = jnp.concatenate([wr_hi, (wr - wr_hi.astype(F32)).astype(BF16)], axis=-1)

    for layer in range(depth):
        x = _mixer_call(
            layer, x, mod, norm_mix_g[layer].reshape(1, d), win, wout, conv_w[layer],
            sgu_norm_g[layer].reshape(1, GROUP_WIDTH),
            jnp.transpose(sgu_w[layer], (1, 0, 2)).reshape(SUPER, HEADS * SUPER),
            jnp.repeat(sgu_b[layer].T, HEAD_DIM, axis=1), wgate[layer], gla_b_gate[layer].reshape(1, -1),
            jnp.tile(gla_norm_g[layer], HEADS).reshape(1, GROUP_WIDTH), hgrn_lower_bounds,
            jnp.tile(hgrn_norm_g[layer], HEADS).reshape(1, GROUP_WIDTH))
        final = layer == depth - 1
        ng = norm_ffn_g[layer].reshape(1, d)
        if layer % 2 == 0:
            x = _ffn_call(final, layer, x, mod, ng, *ffn_w, fng)
        else:
            x = _moe_layer(final, layer, x, mod, ng, wr2[layer // 2], *moe_w, fng)
    return x
```

```python
import functools

import jax
import jax.numpy as jnp
from jax import lax
from jax.experimental import pallas as pl
from jax.experimental.pallas import tpu as pltpu

F32 = jnp.float32
BF16 = jnp.bfloat16
I32 = jnp.int32
EPS = 1e-6

D_MODEL = 1024
N_MOD = 6
GROUP_WIDTH = 256
HEADS = 4
HEAD_DIM = 64
GLA_KEY_DIM = 32
GLA_GATE_RANK = 16
GLA_GATE_NORMALIZER = 16.0
LA_CHUNK_LOG2 = 4
SUPER = 128
SUPER_LOG2 = 7
N_EXPERTS = 8
LANES = 128
BF16_ROWS = 16
BF16_ROWS_LOG2 = 4

C_CB, C_CC, C_CX, C_SU, C_SV = 0, 256, 512, 768, 1024
C_AQ, C_AK, C_AV, C_AG, C_AR = 1280, 1408, 1536, 1792, 1920
C_HQ, C_HF, C_HI, C_HG = 2176, 2432, 2688, 2944
IN_PROJ_PAD = 3200
IN_PROJ_SPLIT = 1808

VMEM_LIMIT = 56 * 1024 * 1024

MIXER_ROWS = 512
FFN_ROWS = 512
MOE_TB = 512
MOE_SLOTS = 2 * MOE_TB + 2 * LANES
MOE_TILE_ROWS = 512
MOE_FF_CHUNK = 512


def _silu(x):
    return x * jax.nn.sigmoid(x)


def _dot(a, b):
    return jnp.dot(a, b, preferred_element_type=F32)


def _dot_nt(a, b):
    return lax.dot_general(a, b, (((1,), (1,)), ((), ())), preferred_element_type=F32)


def _dot_tn(a, b):
    return lax.dot_general(a, b, (((0,), (0,)), ((), ())), preferred_element_type=F32)


def _split_bf16(x, pieces):
    out, r = [], x
    for _ in range(pieces):
        t = r.astype(BF16)
        out.append(t)
        r = r - t.astype(F32)
    return out


def _mod_norm(xt, g, shift, scale):
    ms = jnp.mean(xt * xt, axis=-1, keepdims=True)
    return (xt * lax.rsqrt(ms + EPS)) * g * (1.0 + scale) + shift


def _const_spec(shape):
    zeros = (0,) * len(shape)
    return pl.BlockSpec(shape, lambda *_: zeros, pipeline_mode=pl.Buffered(1))


def _layer_spec(arr, layer):
    zeros = (0,) * (arr.ndim - 1)
    return pl.BlockSpec((None,) + arr.shape[1:], lambda *_: (layer,) + zeros, pipeline_mode=pl.Buffered(1))


def _mod_spec(layer, d):
    return pl.BlockSpec((None, None, N_MOD, d), lambda b, s, *_: (layer, b, 0, 0))


def _ada_kernel(c_ref, w_ref, b_ref, o_ref):
    cond = _silu(c_ref[...])
    o_ref[0] = _dot(cond.astype(BF16), w_ref[0].astype(BF16)) + b_ref[0]


def _ada_call(c, w_ada, b_ada):
    depth, d, n = w_ada.shape
    bsz = c.shape[0]
    tn = n // 4
    return pl.pallas_call(
        _ada_kernel,
        grid=(depth, n // tn),
        in_specs=[pl.BlockSpec((bsz, d), lambda l, j: (0, 0)),
                  pl.BlockSpec((1, d, tn), lambda l, j: (l, 0, j)),
                  pl.BlockSpec((1, 1, tn), lambda l, j: (l, 0, j))],
        out_specs=pl.BlockSpec((1, bsz, tn), lambda l, j: (l, 0, j)),
        out_shape=jax.ShapeDtypeStruct((depth, bsz, n), F32),
        compiler_params=pltpu.CompilerParams(dimension_semantics=("arbitrary", "arbitrary"),
                                             vmem_limit_bytes=VMEM_LIMIT),
        name="adaln_mod",
    )(c, w_ada, b_ada.reshape(depth, 1, n))


def _pair_masks():
    t = lax.broadcasted_iota(I32, (SUPER, HEADS * SUPER), 0)
    s = lax.broadcasted_iota(I32, (SUPER, HEADS * SUPER), 1) & (SUPER - 1)
    masks = [((t >> LA_CHUNK_LOG2) == (s >> LA_CHUNK_LOG2)) & (s <= t)]
    for sh in range(LA_CHUNK_LOG2 + 1, SUPER_LOG2 + 1):
        half = 1 << (sh - 1)
        masks.append(((t >> sh) == (s >> sh)) & ((t & half) != 0) & ((s & half) == 0))
    return masks


def _level_refs(c):
    kt = c.shape[-1]
    n16 = SUPER >> LA_CHUNK_LOG2
    mid = (1 << LA_CHUNK_LOG2) // 2
    refs = [jnp.broadcast_to(c.reshape(n16, 1 << LA_CHUNK_LOG2, kt)[:, mid:mid + 1, :],
                             (n16, 1 << LA_CHUNK_LOG2, kt)).reshape(SUPER, kt)]
    for sh in range(LA_CHUNK_LOG2 + 1, SUPER_LOG2 + 1):
        nb, bs, half = SUPER >> sh, 1 << sh, 1 << (sh - 1)
        refs.append(jnp.broadcast_to(c.reshape(nb, bs, kt)[:, half - 1:half, :], (nb, bs, kt)).reshape(SUPER, kt))
    return refs


def _head_rows(a):
    w = a.shape[-1] // HEADS
    ab = a.astype(BF16)
    lane_head = lax.broadcasted_iota(I32, a.shape, 1) // w
    return jnp.concatenate([jnp.where(lane_head == h, ab, jnp.zeros_like(ab)) for h in range(HEADS)], axis=0)


def _gated_linear_attention_step(q, k, v, log_a, st_ref, ltri, masks):
    kt = q.shape[-1]
    hk = kt // HEADS
    a_hi, a_lo = _split_bf16(log_a, 2)
    c = _dot(ltri, a_hi) + _dot(ltri, a_lo)

    def decay_factors(lvl, cref):
        d = c - cref
        if lvl == 0:
            return jnp.exp(d), jnp.exp(-d)
        e = jnp.exp(-jnp.abs(d))
        return e, e

    scores = None
    for lvl, (cref, mask) in enumerate(zip(_level_refs(c), masks)):
        eq, ek = decay_factors(lvl, cref)
        s_lvl = _dot_nt((q * eq).astype(BF16), _head_rows(k * ek))
        scores = jnp.where(mask, s_lvl, 0.0 if scores is None else scores)
    o = _dot(scores.astype(BF16), _head_rows(v))

    st = st_ref[...]
    o = o + _dot_nt((q * jnp.exp(c)).astype(BF16), st.astype(BF16))
    c_last = c[SUPER - 1:SUPER, :]
    kd = (k * jnp.exp(c_last - c)).astype(BF16)
    upd = _dot_tn(v.astype(BF16), kd)
    diag = (lax.broadcasted_iota(I32, (GROUP_WIDTH, kt), 0) // HEAD_DIM) == (
        lax.broadcasted_iota(I32, (GROUP_WIDTH, kt), 1) // hk)
    st_ref[...] = st * jnp.exp(c_last) + jnp.where(diag, upd, 0.0)
    return o


def _head_rms_gate(o, head_mean, g, gate):
    ms = _dot((o * o).astype(BF16), head_mean)
    return (o * lax.rsqrt(ms + EPS)) * g * _silu(gate)


def _mixer_kernel(layer, hr, x_ref, mod_ref, ng_ref, win_ref, wout_ref, convw_ref, sgug_ref, sguw_ref, sgub_ref,
                  wgate_ref, bgate_ref, glag_ref, hlb_ref, hgng_ref, o_ref,
                  p0_ref, p1_ref, y0_ref, y1_ref, zbuf, st_gla, st_hgrn):
    @pl.when(pl.program_id(1) == 0)
    def _():
        zbuf[:, 0:8, :] = jnp.zeros((2, 8, GROUP_WIDTH), F32)
        st_gla[...] = jnp.zeros_like(st_gla)
        st_hgrn[...] = jnp.zeros_like(st_hgrn)

    halves = ((p0_ref, y0_ref), (p1_ref, y1_ref))
    for hf, (p_ref, _) in enumerate(halves):
        xt = x_ref[hf]
        mod = mod_ref[hf]
        ms = jnp.mean(xt * xt, axis=-1, keepdims=True)
        h = (xt * lax.rsqrt(ms + EPS)) * (ng_ref[...] * (1.0 + mod[1:2])) + mod[0:1]
        p_ref[...] = _dot(h.astype(BF16), win_ref[...])

    masks = _pair_masks()
    ri = lax.broadcasted_iota(I32, (SUPER, SUPER), 0)
    ci = lax.broadcasted_iota(I32, (SUPER, SUPER), 1)
    ltri = jnp.where(ci <= ri, 1.0, 0.0).astype(BF16)
    t_ws = lax.broadcasted_iota(I32, (SUPER, HEADS * SUPER), 0)
    s_ws = lax.broadcasted_iota(I32, (SUPER, HEADS * SUPER), 1) & (SUPER - 1)
    ws = jnp.where(s_ws <= t_ws, sguw_ref[...], 0.0).astype(BF16)
    hrow = lax.broadcasted_iota(I32, (GROUP_WIDTH, GROUP_WIDTH), 0) // HEAD_DIM
    hcol = lax.broadcasted_iota(I32, (GROUP_WIDTH, GROUP_WIDTH), 1) // HEAD_DIM
    head_mean = jnp.where(hrow == hcol, 1.0 / HEAD_DIM, 0.0).astype(BF16)
    group_mean = jnp.full((GROUP_WIDTH, GROUP_WIDTH), 1.0 / GROUP_WIDTH, BF16)
    lb_all = hlb_ref[...]
    lb_e = jnp.exp(lb_all - jnp.max(lb_all, axis=0, keepdims=True))
    lb_sm = lb_e / jnp.sum(lb_e, axis=0, keepdims=True)
    lb = jnp.zeros((1, GROUP_WIDTH), F32)
    for j in range(1, layer + 1):
        lb = lb + lb_sm[j:j + 1]
    cw = convw_ref[...]

    for hf, (p_ref, y_ref) in enumerate(halves):
        z = p_ref[:,C_CC:C_CC + GROUP_WIDTH] * p_ref[:,C_CX:C_CX + GROUP_WIDTH]
        zbuf[hf, 8:8 + hr, :] = z
        conv = cw[0:1] * zbuf[hf, 6:6 + hr, :] + cw[1:2] * zbuf[hf, 7:7 + hr, :] + cw[2:3] * z
        y_ref[:, 0:GROUP_WIDTH] = (p_ref[:,C_CB:C_CB + GROUP_WIDTH] * conv).astype(BF16)
        zbuf[hf, 0:8, :] = zbuf[hf, hr:hr + 8, :]

        sv = p_ref[:,C_SV:C_SV + GROUP_WIDTH]
        sv_hi, sv_lo = _split_bf16(sv, 2)
        dv = sv - (_dot(sv_hi, group_mean) + _dot(sv_lo, group_mean))
        var = _dot((dv * dv).astype(BF16), group_mean)
        vn = (dv * lax.rsqrt(var + EPS)) * sgug_ref[...]
        gate_logit = _dot(p_ref[:,C_AG:C_AG + LANES].astype(BF16), wgate_ref[...]) + bgate_ref[...]
        log_a = (jnp.minimum(gate_logit, 0.0) - jnp.log(1.0 + jnp.exp(-jnp.abs(gate_logit)))) * (
            1.0 / GLA_GATE_NORMALIZER)
        f = lb + (1.0 - lb) * jax.nn.sigmoid(p_ref[:,C_HF:C_HF + GROUP_WIDTH])
        log_f = jnp.log(f)

        o_gla, o_hgrn = [], []
        for i in range(hr // SUPER):
            r = slice(i * SUPER, (i + 1) * SUPER)
            mixed = _dot(ws, _head_rows(vn[r])) + sgub_ref[...]
            y_ref[r, GROUP_WIDTH:2 * GROUP_WIDTH] = (p_ref[r,C_SU:C_SU + GROUP_WIDTH] * mixed).astype(BF16)
            o_gla.append(_gated_linear_attention_step(
                p_ref[r,C_AQ:C_AQ + LANES] * (GLA_KEY_DIM ** -0.5), p_ref[r,C_AK:C_AK + LANES],
                p_ref[r,C_AV:C_AV + GROUP_WIDTH], log_a[r], st_gla.at[hf], ltri, masks))
            o_hgrn.append(_gated_linear_attention_step(
                p_ref[r,C_HQ:C_HQ + GROUP_WIDTH], 1.0 - f[r], p_ref[r,C_HI:C_HI + GROUP_WIDTH], log_f[r],
                st_hgrn.at[hf], ltri, masks))
        y_ref[:, 2 * GROUP_WIDTH:3 * GROUP_WIDTH] = _head_rms_gate(
            jnp.concatenate(o_gla, axis=0), head_mean, glag_ref[...], p_ref[:,C_AR:C_AR + GROUP_WIDTH]).astype(BF16)
        y_ref[:, 3 * GROUP_WIDTH:4 * GROUP_WIDTH] = _head_rms_gate(
            jnp.concatenate(o_hgrn, axis=0), head_mean, hgng_ref[...], p_ref[:,C_HG:C_HG + GROUP_WIDTH]).astype(BF16)

        o_ref[hf] = x_ref[hf] + mod_ref[hf][2:3] * _dot(y_ref[...], wout_ref[...])


def _mixer_call(layer, x, mod, ng, win, wout, convw, sgug, sguw, sgub, wgate, bgate, glag, hlb, hgng):
    bsz, seq, d = x.shape
    hr = MIXER_ROWS
    assert bsz % 2 == 0 and seq % hr == 0
    small = [convw, sgug, sguw, sgub, wgate, bgate, glag, hlb, hgng]
    return pl.pallas_call(
        functools.partial(_mixer_kernel, layer, hr),
        grid=(bsz // 2, seq // hr),
        in_specs=[pl.BlockSpec((2, hr, d), lambda b, s: (b, s, 0)),
                  pl.BlockSpec((None, 2, N_MOD, d), lambda b, s: (layer, b, 0, 0)), _const_spec(ng.shape),
                  _layer_spec(win, layer), _layer_spec(wout, layer)] + [_const_spec(a.shape) for a in small],
        out_specs=pl.BlockSpec((2, hr, d), lambda b, s: (b, s, 0)),
        out_shape=jax.ShapeDtypeStruct(x.shape, F32),
        scratch_shapes=[pltpu.VMEM((hr, IN_PROJ_PAD), F32), pltpu.VMEM((hr, IN_PROJ_PAD), F32),
                        pltpu.VMEM((hr, d), BF16), pltpu.VMEM((hr, d), BF16),
                        pltpu.VMEM((2, hr + 8, GROUP_WIDTH), F32),
                        pltpu.VMEM((2, GROUP_WIDTH, HEADS * GLA_KEY_DIM), F32),
                        pltpu.VMEM((2, GROUP_WIDTH, GROUP_WIDTH), F32)],
        input_output_aliases={0: 0} if layer > 0 else {},
        compiler_params=pltpu.CompilerParams(dimension_semantics=("parallel", "arbitrary"),
                                             vmem_limit_bytes=VMEM_LIMIT),
        name=f"mixer_l{layer}",
    )(x, mod, ng, win, wout, *small)


def _final_norm(out, g):
    ms = jnp.mean(out * out, axis=-1, keepdims=True)
    return (out * lax.rsqrt(ms + EPS)) * g


def _ffn_kernel(final, x_ref, mod_ref, ng_ref, w1_ref, w3_ref, w2_ref, fng_ref, o_ref):
    xt = x_ref[0]
    mod = mod_ref[...]
    hb = _mod_norm(xt, ng_ref[...], mod[3:4], mod[4:5]).astype(BF16)
    h1 = _dot(hb, w1_ref[...])
    h3 = _dot(hb, w3_ref[...])
    a = (_silu(h1) * h3).astype(BF16)
    out = xt + mod[5:6] * _dot(a, w2_ref[...])
    o_ref[0] = _final_norm(out, fng_ref[...]) if final else out


def _ffn_call(final, layer, x, mod, ng, w1, w3, w2, fng):
    bsz, seq, d = x.shape
    tm = FFN_ROWS
    idx = layer // 2
    return pl.pallas_call(
        functools.partial(_ffn_kernel, final),
        grid=(bsz, seq // tm),
        in_specs=[pl.BlockSpec((1, tm, d), lambda b, s: (b, s, 0)), _mod_spec(layer, d), _const_spec(ng.shape),
                  _layer_spec(w1, idx), _layer_spec(w3, idx), _layer_spec(w2, idx), _const_spec(fng.shape)],
        out_specs=pl.BlockSpec((1, tm, d), lambda b, s: (b, s, 0)),
        out_shape=jax.ShapeDtypeStruct(x.shape, F32),
        input_output_aliases={0: 0},
        compiler_params=pltpu.CompilerParams(dimension_semantics=("parallel", "arbitrary"),
                                             vmem_limit_bytes=VMEM_LIMIT),
        name="dense_ffn",
    )(x, mod, ng, w1, w3, w2, fng)


SEG_SIZES = tuple(BF16_ROWS << i for i in range(5, -1, -1))


def _segment_copies(fn, buf_off, hbm_off, nrows):
    pos = jnp.int32(0)
    for size in SEG_SIZES:
        fn(pl.multiple_of(buf_off + pos, BF16_ROWS), pl.multiple_of(hbm_off + pos, BF16_ROWS), size,
           (nrows & size) != 0)
        pos = pos + (nrows & size)


def _route_kernel(nsb, nblk, x_ref, mod_ref, ng_ref, wr_ref, sut_ref, xs_hbm, route_ref, seg_ref,
                  xs_buf, carry_buf, run_ref, prev_ref, sem):
    tb, slots = MOE_TB, MOE_SLOTS
    blk = pl.program_id(0) * nsb + pl.program_id(1)

    @pl.when(blk == 0)
    def _():
        carry_buf[...] = jnp.zeros_like(carry_buf)
        for e in range(N_EXPERTS):
            run_ref[e] = 0
        for i in range(3 * N_EXPERTS):
            prev_ref[i] = 0

    mod = mod_ref[...]
    h = _mod_norm(x_ref[0], ng_ref[...], mod[3:4], mod[4:5])
    h_hi, h_lo = _split_bf16(h, 2)
    lg = _dot(h_hi, wr_ref[...]) + _dot(h_lo, wr_ref[...])
    logits = lg[:, 0:LANES] + lg[:, LANES:2 * LANES]
    lt = logits.T[0:N_EXPERTS, :]
    eid = lax.broadcasted_iota(I32, (N_EXPERTS, tb), 0)
    m1 = jnp.max(lt, axis=0, keepdims=True)
    i1 = jnp.min(jnp.where(lt == m1, eid, N_EXPERTS), axis=0, keepdims=True)
    lt2 = jnp.where(eid == i1, -jnp.inf, lt)
    m2 = jnp.max(lt2, axis=0, keepdims=True)
    i2 = jnp.min(jnp.where(lt2 == m2, eid, N_EXPERTS), axis=0, keepdims=True)
    e2 = jnp.exp(m2 - m1)
    w_first = 1.0 / (1.0 + e2)
    w_second = e2 / (1.0 + e2)

    sel1, sel2 = eid == i1, eid == i2
    onehot = jnp.where(sel1 | sel2, 1.0, 0.0).astype(F32)
    rank = _dot(onehot.astype(BF16), sut_ref[...])
    counts = [jnp.sum(onehot[e:e + 1, :]).astype(I32) for e in range(N_EXPERTS)]
    runs = [run_ref[e] for e in range(N_EXPERTS)]
    rem = [r & (BF16_ROWS - 1) for r in runs]
    base = [r - m for r, m in zip(runs, rem)]
    width = [m + n for m, n in zip(rem, counts)]
    win16 = [((w + (BF16_ROWS - 1)) >> BF16_ROWS_LOG2) << BF16_ROWS_LOG2 for w in width]
    nfull = [(w >> BF16_ROWS_LOG2) << BF16_ROWS_LOG2 for w in width]
    left = [w - f for w, f in zip(width, nfull)]
    offs, total = [], jnp.int32(0)
    for e in range(N_EXPERTS):
        offs.append(total)
        total = total + win16[e]

    sub = lax.broadcasted_iota(I32, (N_EXPERTS, 1), 0)

    def expert_column(vals):
        col = jnp.zeros((N_EXPERTS, 1), I32)
        for e in range(N_EXPERTS):
            col = jnp.where(sub == e, vals[e], col)
        return col

    off_col, win_col, base_col = expert_column(offs), expert_column(win16), expert_column(base)
    first_col = expert_column([o + m for o, m in zip(offs, rem)])
    slot_all = first_col.astype(F32) + rank
    slot1 = jnp.sum(jnp.where(sel1, slot_all, 0.0), axis=0, keepdims=True)
    slot2 = jnp.sum(jnp.where(sel2, slot_all, 0.0), axis=0, keepdims=True)

    rowi = lax.broadcasted_iota(I32, (slots, tb), 0)
    perm = jnp.where((rowi == slot1.astype(I32)) | (rowi == slot2.astype(I32)), 1.0, 0.0).astype(BF16)
    xs_sorted = _dot(perm, h_hi).astype(BF16)

    def copy(e, buf_row, hbm_row, size):
        return pltpu.make_async_copy(xs_buf.at[pl.ds(buf_row, size), :], xs_hbm.at[e, pl.ds(hbm_row, size), :], sem)

    def start(e):
        def fn(buf_row, hbm_row, size, cond):
            @pl.when(cond)
            def _():
                copy(e, buf_row, hbm_row, size).start()
        return fn

    def wait(e):
        def fn(buf_row, hbm_row, size, cond):
            @pl.when(cond)
            def _():
                copy(e, buf_row, hbm_row, size).wait()
        return fn

    for e in range(N_EXPERTS):
        _segment_copies(wait(e), prev_ref[e], prev_ref[2 * N_EXPERTS + e], prev_ref[N_EXPERTS + e])
    xs_buf[...] = xs_sorted
    for e in range(N_EXPERTS):
        head = pl.ds(pl.multiple_of(offs[e], BF16_ROWS), BF16_ROWS)
        xs_buf[head, :] = xs_buf[head, :] + carry_buf[e]
    for e in range(N_EXPERTS):
        _segment_copies(start(e), offs[e], base[e], nfull[e])
    for e in range(N_EXPERTS):
        tail = xs_buf[pl.ds(pl.multiple_of(offs[e] + nfull[e], BF16_ROWS), BF16_ROWS), :]
        carry_buf[e] = jnp.where(left[e] > 0, tail, jnp.zeros_like(tail))

    @pl.when(blk == nblk - 1)
    def _():
        for e in range(N_EXPERTS):
            _segment_copies(wait(e), offs[e], base[e], nfull[e])
        for e in range(N_EXPERTS):
            @pl.when(left[e] > 0)
            def _(e=e):
                flush = pltpu.make_async_copy(
                    carry_buf.at[e],
                    xs_hbm.at[e, pl.ds(pl.multiple_of(base[e] + nfull[e], BF16_ROWS), BF16_ROWS), :], sem)
                flush.start()
                flush.wait()

    route_ref[0] = jnp.concatenate([slot1, slot2, w_first, w_second, jnp.zeros((4, tb), F32)], axis=0)
    lane = lax.broadcasted_iota(I32, (N_EXPERTS, LANES), 1)
    after_col = expert_column([r + n for r, n in zip(runs, counts)])
    seg_ref[0] = jnp.where(lane == 0, off_col, jnp.where(lane == 1, win_col, jnp.where(
        lane == 2, base_col, jnp.where(lane == 3, after_col, 0))))
    for e in range(N_EXPERTS):
        run_ref[e] = runs[e] + counts[e]
        prev_ref[e] = offs[e]
        prev_ref[N_EXPERTS + e] = nfull[e]
        prev_ref[2 * N_EXPERTS + e] = base[e]


def _route_call(layer, x, mod, ng, wr2, cap):
    bsz, seq, d = x.shape
    tb = MOE_TB
    nsb = seq // tb
    nblk = bsz * nsb
    ri = lax.broadcasted_iota(I32, (tb, tb), 0)
    ci = lax.broadcasted_iota(I32, (tb, tb), 1)
    sut = (ri < ci).astype(BF16)
    return pl.pallas_call(
        functools.partial(_route_kernel, nsb, nblk),
        grid=(bsz, nsb),
        in_specs=[pl.BlockSpec((1, tb, d), lambda b, s: (b, s, 0)), _mod_spec(layer, d),
                  _const_spec(ng.shape), _const_spec(wr2.shape), _const_spec((tb, tb))],
        out_specs=[pl.BlockSpec(memory_space=pl.ANY),
                   pl.BlockSpec((1, 8, tb), lambda b, s: (b * nsb + s, 0, 0)),
                   pl.BlockSpec((1, N_EXPERTS, LANES), lambda b, s: (b * nsb + s, 0, 0))],
        out_shape=[jax.ShapeDtypeStruct((N_EXPERTS, cap, d), BF16),
                   jax.ShapeDtypeStruct((nblk, 8, tb), F32),
                   jax.ShapeDtypeStruct((nblk, N_EXPERTS, LANES), I32)],
        scratch_shapes=[pltpu.VMEM((MOE_SLOTS, d), BF16), pltpu.VMEM((N_EXPERTS, BF16_ROWS, d), BF16),
                        pltpu.SMEM((N_EXPERTS,), I32), pltpu.SMEM((3 * N_EXPERTS,), I32), pltpu.SemaphoreType.DMA],
        compiler_params=pltpu.CompilerParams(dimension_semantics=("arbitrary", "arbitrary"),
                                             vmem_limit_bytes=VMEM_LIMIT),
        name="moe_route",
    )(x, mod, ng, wr2, sut)


def _gmm_kernel(idx, te_ref, trb_ref, tval_ref, tnv_ref, xs_ref, w1_hbm, w3_hbm, w2_hbm, ys_ref,
                w1_ref, w3_ref, w2_ref, stage_in, stage_out, sem):
    i = pl.program_id(0)
    e = te_ref[i]
    tf = MOE_FF_CHUNK
    nf = w1_ref.shape[-1] // tf

    def chunk_copies(j, slot):
        return (pltpu.make_async_copy(w1_hbm.at[idx, e, :, pl.ds(j * tf, tf)], stage_in.at[slot, 0], sem.at[slot, 0]),
                pltpu.make_async_copy(w3_hbm.at[idx, e, :, pl.ds(j * tf, tf)], stage_in.at[slot, 1], sem.at[slot, 1]),
                pltpu.make_async_copy(w2_hbm.at[idx, e, pl.ds(j * tf, tf), :], stage_out.at[slot], sem.at[slot, 2]))

    def tile(load):
        tm = xs_ref.shape[0]
        rows = lax.broadcasted_iota(I32, (tm, 1), 0)
        xs = xs_ref[...]
        xs = jnp.where(rows < tnv_ref[i], xs, jnp.zeros_like(xs))
        if load:
            for cp in chunk_copies(0, 0):
                cp.start()
        acc = None
        for j in range(nf):
            cols = slice(j * tf, (j + 1) * tf)
            if load:
                slot = j % 2
                if j + 1 < nf:
                    for cp in chunk_copies(j + 1, 1 - slot):
                        cp.start()
                for cp in chunk_copies(j, slot):
                    cp.wait()
                w1_ref[:, cols] = stage_in[slot, 0].astype(BF16)
                w3_ref[:, cols] = stage_in[slot, 1].astype(BF16)
                w2_ref[cols, :] = stage_out[slot].astype(BF16)
            h1 = _dot(xs, w1_ref[:, cols])
            h3 = _dot(xs, w3_ref[:, cols])
            part = _dot((_silu(h1) * h3).astype(BF16), w2_ref[cols, :])
            acc = part if acc is None else acc + part
        ys_ref[...] = acc.astype(BF16)

    valid = tval_ref[i] == 1
    first = (i == 0) | (e != te_ref[jnp.maximum(i - 1, 0)])

    @pl.when(valid & first)
    def _():
        tile(True)

    @pl.when(valid & jnp.logical_not(first))
    def _():
        tile(False)


def _gmm_call(idx, xs, w1, w3, w2, te, trb, tval, tnv):
    ne, cap, d = xs.shape
    dff = w1.shape[-1]
    tm = MOE_TILE_ROWS
    nt = te.shape[0]
    anywhere = pl.BlockSpec(memory_space=pl.ANY)

    return pl.pallas_call(
        functools.partial(_gmm_kernel, idx),
        grid_spec=pltpu.PrefetchScalarGridSpec(
            num_scalar_prefetch=4,
            grid=(nt,),
            in_specs=[pl.BlockSpec((None, tm, d), lambda i, te, trb, tval, tnv: (te[i], trb[i], 0)),
                      anywhere, anywhere, anywhere],
            out_specs=pl.BlockSpec((None, tm, d), lambda i, te, trb, tval, tnv: (te[i], trb[i], 0)),
            scratch_shapes=[pltpu.VMEM((d, dff), BF16), pltpu.VMEM((d, dff), BF16), pltpu.VMEM((dff, d), BF16),
                            pltpu.VMEM((2, 2, d, MOE_FF_CHUNK), F32), pltpu.VMEM((2, MOE_FF_CHUNK, d), F32),
                            pltpu.SemaphoreType.DMA((2, 3))]),
        out_shape=jax.ShapeDtypeStruct((ne, cap, d), BF16),
        compiler_params=pltpu.CompilerParams(dimension_semantics=("arbitrary",), vmem_limit_bytes=VMEM_LIMIT),
        name="moe_experts",
    )(te, trb, tval, tnv, xs, w1, w3, w2)


def _combine_kernel(final, nsb, nblk, seg_ref, x_ref, mod_ref, route_ref, fng_ref, ys_hbm, o_ref, ys_buf, sem):
    tb, slots = MOE_TB, MOE_SLOTS
    blk = pl.program_id(0) * nsb + pl.program_id(1)
    cur = blk & 1

    def segments(b):
        base = b * (3 * N_EXPERTS)
        return ([seg_ref[base + e] for e in range(N_EXPERTS)],
                [seg_ref[base + N_EXPERTS + e] for e in range(N_EXPERTS)],
                [seg_ref[base + 2 * N_EXPERTS + e] for e in range(N_EXPERTS)])

    def fetch(b, buf, also, wait):
        offs, cnt16, runs = segments(b)
        for e in range(N_EXPERTS):
            def fn(buf_row, hbm_row, size, cond, e=e):
                @pl.when(cond & also)
                def _():
                    cp = pltpu.make_async_copy(ys_hbm.at[e, pl.ds(hbm_row, size), :],
                                               ys_buf.at[buf, pl.ds(buf_row, size), :], sem.at[buf])
                    cp.wait() if wait else cp.start()
            _segment_copies(fn, offs[e], runs[e], cnt16[e])
        return offs[N_EXPERTS - 1] + cnt16[N_EXPERTS - 1]

    @pl.when(blk == 0)
    def _():
        fetch(blk, cur, True, wait=False)

    fetch(jnp.minimum(blk + 1, nblk - 1), 1 - cur, blk + 1 < nblk, wait=False)
    total = fetch(blk, cur, True, wait=True)

    rowi = lax.broadcasted_iota(I32, (slots, 1), 0)
    ys = ys_buf[cur]
    ys = jnp.where(rowi < total, ys, jnp.zeros_like(ys))

    route = jnp.concatenate([route_ref[0], jnp.zeros((LANES - 8, tb), F32)], axis=0).T
    slot1, slot2 = route[:, 0:1].astype(I32), route[:, 1:2].astype(I32)
    lane = lax.broadcasted_iota(I32, (tb, slots), 1)
    comb = (jnp.where(lane == slot1, route[:, 2:3], 0.0) + jnp.where(lane == slot2, route[:, 3:4], 0.0)).astype(BF16)
    out = x_ref[0] + mod_ref[...][5:6] * _dot(comb, ys)
    o_ref[0] = _final_norm(out, fng_ref[...]) if final else out


def _combine_call(final, layer, x, mod, route, seg_flat, ys, fng):
    bsz, seq, d = x.shape
    tb = MOE_TB
    nsb = seq // tb
    return pl.pallas_call(
        functools.partial(_combine_kernel, final, nsb, bsz * nsb),
        grid_spec=pltpu.PrefetchScalarGridSpec(
            num_scalar_prefetch=1,
            grid=(bsz, nsb),
            in_specs=[pl.BlockSpec((1, tb, d), lambda b, s, seg: (b, s, 0)), _mod_spec(layer, d),
                      pl.BlockSpec((1, 8, tb), lambda b, s, seg: (b * nsb + s, 0, 0)),
                      _const_spec(fng.shape),
                      pl.BlockSpec(memory_space=pl.ANY)],
            out_specs=pl.BlockSpec((1, tb, d), lambda b, s, seg: (b, s, 0)),
            scratch_shapes=[pltpu.VMEM((2, MOE_SLOTS, d), BF16), pltpu.SemaphoreType.DMA((2,))]),
        out_shape=jax.ShapeDtypeStruct(x.shape, F32),
        input_output_aliases={1: 0},
        compiler_params=pltpu.CompilerParams(dimension_semantics=("arbitrary", "arbitrary"),
                                             vmem_limit_bytes=VMEM_LIMIT),
        name="moe_combine",
    )(seg_flat, x, mod, route, fng, ys)


def _moe_tiles(seg, tm, nt):
    total = seg[-1, :, 3]
    tiles = (total + tm - 1) // tm
    ends = jnp.cumsum(tiles)
    starts = ends - tiles
    ntiles = ends[-1]
    i = jnp.arange(nt, dtype=I32)
    valid = i < ntiles
    ie = jnp.minimum(i, ntiles - 1)
    te = jnp.minimum(jnp.sum((ie[:, None] >= ends[None, :]).astype(I32), axis=1), N_EXPERTS - 1)
    trb = ie - starts[te]
    tnv = jnp.clip(total[te] - trb * tm, 0, tm)
    return te.astype(I32), trb.astype(I32), valid.astype(I32), tnv.astype(I32)


def _moe_layer(final, layer, x, mod, ng, wr2, w1, w3, w2, fng):
    bsz, seq, d = x.shape
    tokens = bsz * seq
    nblk = tokens // MOE_TB
    tm = MOE_TILE_ROWS
    cap = -(-(tokens + BF16_ROWS) // tm) * tm
    nt = -(-2 * tokens // tm) + N_EXPERTS
    xs, route, seg = _route_call(layer, x, mod, ng, wr2, cap)
    te, trb, tval, tnv = _moe_tiles(seg, tm, nt)
    ys = _gmm_call(layer // 2, xs, w1, w3, w2, te, trb, tval, tnv)
    seg_flat = jnp.transpose(seg[:, :, 0:3], (0, 2, 1)).reshape(-1)
    return _combine_call(final, layer, x, mod, route, seg_flat, ys, fng)


def kernel(x, c, norm_mix_g, norm_ffn_g, final_norm_g, w_ada, b_ada, w_in, w_out, conv_w, sgu_norm_g, sgu_w, sgu_b,
           gla_w_gate, gla_b_gate, gla_norm_g, hgrn_lower_bounds, hgrn_norm_g, ffn_w1, ffn_w3, ffn_w2, moe_router,
           moe_w1, moe_w3, moe_w2):
    depth = w_in.shape[0]
    bsz, seq, d = x.shape
    assert d == D_MODEL and seq % MOE_TB == 0 and hgrn_lower_bounds.shape[0] == depth

    mod = _ada_call(c, w_ada, b_ada).reshape(depth, bsz, N_MOD, d)

    win = jnp.concatenate([w_in[:, :, :IN_PROJ_SPLIT],
                           jnp.zeros((depth, d, LANES - GLA_GATE_RANK), w_in.dtype),
                           w_in[:, :, IN_PROJ_SPLIT:]], axis=-1).astype(BF16)
    wout = w_out.astype(BF16)
    wgate = jnp.concatenate([gla_w_gate, jnp.zeros((depth, LANES - GLA_GATE_RANK, gla_w_gate.shape[-1]), F32)],
                            axis=1).astype(BF16)
    fng = final_norm_g.reshape(1, d)
    ffn_w = [w.astype(BF16) for w in (ffn_w1, ffn_w3, ffn_w2)]
    moe_w = (moe_w1, moe_w3, moe_w2)
    wr = jnp.concatenate([moe_router, jnp.zeros(moe_router.shape[:2] + (LANES - N_EXPERTS,), F32)], axis=-1)
    wr_hi = wr.astype(BF16)
    wr2 = jnp.concatenate([wr_hi, (wr - wr_hi.astype(F32)).astype(BF16)], axis=-1)

    for layer in range(depth):
        x = _mixer_call(
            layer, x, mod, norm_mix_g[layer].reshape(1, d), win, wout, conv_w[layer],
            sgu_norm_g[layer].reshape(1, GROUP_WIDTH),
            jnp.transpose(sgu_w[layer], (1, 0, 2)).reshape(SUPER, HEADS * SUPER),
            jnp.repeat(sgu_b[layer].T, HEAD_DIM, axis=1), wgate[layer], gla_b_gate[layer].reshape(1, -1),
            jnp.tile(gla_norm_g[layer], HEADS).reshape(1, GROUP_WIDTH), hgrn_lower_bounds,
            jnp.tile(hgrn_norm_g[layer], HEADS).reshape(1, GROUP_WIDTH))
        final = layer == depth - 1
        ng = norm_ffn_g[layer].reshape(1, d)
        if layer % 2 == 0:
            x = _ffn_call(final, layer, x, mod, ng, *ffn_w, fng)
        else:
            x = _moe_layer(final, layer, x, mod, ng, wr2[layer // 2], *moe_w, fng)
    return x
```

```python
import functools

import jax
import jax.numpy as jnp
from jax import lax
from jax.experimental import pallas as pl
from jax.experimental.pallas import tpu as pltpu

F32 = jnp.float32
BF16 = jnp.bfloat16
I32 = jnp.int32
EPS = 1e-6

D_MODEL = 1024
N_MOD = 6
GROUP_WIDTH = 256
HEADS = 4
HEAD_DIM = 64
GLA_KEY_DIM = 32
GLA_GATE_RANK = 16
GLA_GATE_NORMALIZER = 16.0
LA_CHUNK_LOG2 = 4
SUPER = 128
SUPER_LOG2 = 7
N_EXPERTS = 8
LANES = 128
F32_ROWS = 8
BF16_ROWS = 16
BF16_ROWS_LOG2 = 4

C_CB, C_CC, C_CX, C_SU, C_SV = 0, 256, 512, 768, 1024
C_AQ, C_AK, C_AV, C_AG, C_AR = 1280, 1408, 1536, 1792, 1920
C_HQ, C_HF, C_HI, C_HG = 2176, 2432, 2688, 2944
IN_PROJ_PAD = 3200
IN_PROJ_SPLIT = 1808

VMEM_LIMIT = 56 * 1024 * 1024

MIXER_ROWS = 512
FFN_ROWS = 512
MOE_TB = 512
MOE_SLOTS = 2 * MOE_TB + 2 * LANES
MOE_TILE_ROWS = 512
MOE_FF_CHUNK = 512


def _silu(x):
    return x * jax.nn.sigmoid(x)


def _dot(a, b):
    return jnp.dot(a, b, preferred_element_type=F32)


def _dot_nt(a, b):
    return lax.dot_general(a, b, (((1,), (1,)), ((), ())), preferred_element_type=F32)


def _dot_tn(a, b):
    return lax.dot_general(a, b, (((0,), (0,)), ((), ())), preferred_element_type=F32)


def _split_bf16(x, pieces):
    out, r = [], x
    for _ in range(pieces):
        t = r.astype(BF16)
        out.append(t)
        r = r - t.astype(F32)
    return out


def _mod_norm(xt, g, shift, scale):
    ms = jnp.mean(xt * xt, axis=-1, keepdims=True)
    return (xt * lax.rsqrt(ms + EPS)) * g * (1.0 + scale) + shift


def _const_spec(shape):
    zeros = (0,) * len(shape)
    return pl.BlockSpec(shape, lambda *_: zeros, pipeline_mode=pl.Buffered(1))


def _layer_spec(arr, layer):
    zeros = (0,) * (arr.ndim - 1)
    return pl.BlockSpec((None,) + arr.shape[1:], lambda *_: (layer,) + zeros, pipeline_mode=pl.Buffered(1))


def _mod_spec(layer, d):
    return pl.BlockSpec((None, None, N_MOD, d), lambda b, s, *_: (layer, b, 0, 0))


def _ada_kernel(c_ref, w_ref, b_ref, o_ref):
    cond = _silu(c_ref[...])
    o_ref[0] = _dot(cond.astype(BF16), w_ref[0].astype(BF16)) + b_ref[0]


def _ada_call(c, w_ada, b_ada):
    depth, d, n = w_ada.shape
    bsz = c.shape[0]
    tn = n // 4
    return pl.pallas_call(
        _ada_kernel,
        grid=(depth, n // tn),
        in_specs=[pl.BlockSpec((bsz, d), lambda l, j: (0, 0)),
                  pl.BlockSpec((1, d, tn), lambda l, j: (l, 0, j)),
                  pl.BlockSpec((1, 1, tn), lambda l, j: (l, 0, j))],
        out_specs=pl.BlockSpec((1, bsz, tn), lambda l, j: (l, 0, j)),
        out_shape=jax.ShapeDtypeStruct((depth, bsz, n), F32),
        compiler_params=pltpu.CompilerParams(dimension_semantics=("arbitrary", "arbitrary"),
                                             vmem_limit_bytes=VMEM_LIMIT),
        name="adaln_mod",
    )(c, w_ada, b_ada.reshape(depth, 1, n))


def _pair_masks():
    t = lax.broadcasted_iota(I32, (SUPER, HEADS * SUPER), 0)
    s = lax.broadcasted_iota(I32, (SUPER, HEADS * SUPER), 1) & (SUPER - 1)
    masks = [((t >> LA_CHUNK_LOG2) == (s >> LA_CHUNK_LOG2)) & (s <= t)]
    for sh in range(LA_CHUNK_LOG2 + 1, SUPER_LOG2 + 1):
        half = 1 << (sh - 1)
        masks.append(((t >> sh) == (s >> sh)) & ((t & half) != 0) & ((s & half) == 0))
    return masks


def _level_refs(c):
    kt = c.shape[-1]
    n16 = SUPER >> LA_CHUNK_LOG2
    mid = (1 << LA_CHUNK_LOG2) // 2
    refs = [jnp.broadcast_to(c.reshape(n16, 1 << LA_CHUNK_LOG2, kt)[:, mid:mid + 1, :],
                             (n16, 1 << LA_CHUNK_LOG2, kt)).reshape(SUPER, kt)]
    for sh in range(LA_CHUNK_LOG2 + 1, SUPER_LOG2 + 1):
        nb, bs, half = SUPER >> sh, 1 << sh, 1 << (sh - 1)
        refs.append(jnp.broadcast_to(c.reshape(nb, bs, kt)[:, half - 1:half, :], (nb, bs, kt)).reshape(SUPER, kt))
    return refs


def _head_rows(a):
    w = a.shape[-1] // HEADS
    ab = a.astype(BF16)
    lane_head = lax.broadcasted_iota(I32, a.shape, 1) // w
    return jnp.concatenate([jnp.where(lane_head == h, ab, jnp.zeros_like(ab)) for h in range(HEADS)], axis=0)


def _gated_linear_attention_step(q, k, v, log_a, st_ref, ltri, masks):
    kt = q.shape[-1]
    hk = kt // HEADS
    a_hi, a_lo = _split_bf16(log_a, 2)
    c = _dot(ltri, a_hi) + _dot(ltri, a_lo)

    def decay_factors(lvl, cref):
        d = c - cref
        if lvl == 0:
            return jnp.exp(d), jnp.exp(-d)
        e = jnp.exp(-jnp.abs(d))
        return e, e

    scores = None
    for lvl, (cref, mask) in enumerate(zip(_level_refs(c), masks)):
        eq, ek = decay_factors(lvl, cref)
        s_lvl = _dot_nt((q * eq).astype(BF16), _head_rows(k * ek))
        scores = jnp.where(mask, s_lvl, 0.0 if scores is None else scores)
    o = _dot(scores.astype(BF16), _head_rows(v))

    st = st_ref[...]
    o = o + _dot_nt((q * jnp.exp(c)).astype(BF16), st.astype(BF16))
    c_last = c[SUPER - 1:SUPER, :]
    kd = (k * jnp.exp(c_last - c)).astype(BF16)
    upd = _dot_tn(v.astype(BF16), kd)
    diag = (lax.broadcasted_iota(I32, (GROUP_WIDTH, kt), 0) // HEAD_DIM) == (
        lax.broadcasted_iota(I32, (GROUP_WIDTH, kt), 1) // hk)
    st_ref[...] = st * jnp.exp(c_last) + jnp.where(diag, upd, 0.0)
    return o


def _head_rms_gate(o, head_mean, g, gate):
    ms = _dot((o * o).astype(BF16), head_mean)
    return (o * lax.rsqrt(ms + EPS)) * g * _silu(gate)


def _mixer_kernel(layer, hr, x_ref, mod_ref, ng_ref, win_ref, wout_ref, convw_ref, sgug_ref, sguw_ref, sgub_ref,
                  wgate_ref, bgate_ref, glag_ref, hlb_ref, hgng_ref, o_ref,
                  p0_ref, p1_ref, y0_ref, y1_ref, zbuf, st_gla, st_hgrn):
    @pl.when(pl.program_id(1) == 0)
    def _():
        zbuf[:, 0:F32_ROWS, :] = jnp.zeros((2, F32_ROWS, GROUP_WIDTH), F32)
        st_gla[...] = jnp.zeros_like(st_gla)
        st_hgrn[...] = jnp.zeros_like(st_hgrn)

    halves = ((p0_ref, y0_ref), (p1_ref, y1_ref))
    for hf, (p_ref, _) in enumerate(halves):
        xt = x_ref[hf]
        mod = mod_ref[hf]
        ms = jnp.mean(xt * xt, axis=-1, keepdims=True)
        h = (xt * lax.rsqrt(ms + EPS)) * (ng_ref[...] * (1.0 + mod[1:2])) + mod[0:1]
        p_ref[...] = _dot(h.astype(BF16), win_ref[...])

    masks = _pair_masks()
    ri = lax.broadcasted_iota(I32, (SUPER, SUPER), 0)
    ci = lax.broadcasted_iota(I32, (SUPER, SUPER), 1)
    ltri = jnp.where(ci <= ri, 1.0, 0.0).astype(BF16)
    t_ws = lax.broadcasted_iota(I32, (SUPER, HEADS * SUPER), 0)
    s_ws = lax.broadcasted_iota(I32, (SUPER, HEADS * SUPER), 1) & (SUPER - 1)
    ws = jnp.where(s_ws <= t_ws, sguw_ref[...], 0.0).astype(BF16)
    hrow = lax.broadcasted_iota(I32, (GROUP_WIDTH, GROUP_WIDTH), 0) // HEAD_DIM
    hcol = lax.broadcasted_iota(I32, (GROUP_WIDTH, GROUP_WIDTH), 1) // HEAD_DIM
    head_mean = jnp.where(hrow == hcol, 1.0 / HEAD_DIM, 0.0).astype(BF16)
    group_mean = jnp.full((GROUP_WIDTH, GROUP_WIDTH), 1.0 / GROUP_WIDTH, BF16)
    lb_all = hlb_ref[...]
    lb_e = jnp.exp(lb_all - jnp.max(lb_all, axis=0, keepdims=True))
    lb_sm = lb_e / jnp.sum(lb_e, axis=0, keepdims=True)
    lb = jnp.zeros((1, GROUP_WIDTH), F32)
    for j in range(1, layer + 1):
        lb = lb + lb_sm[j:j + 1]
    cw = convw_ref[...]

    for hf, (p_ref, y_ref) in enumerate(halves):
        z = p_ref[:, C_CC:C_CC + GROUP_WIDTH] * p_ref[:, C_CX:C_CX + GROUP_WIDTH]
        zbuf[hf, F32_ROWS:F32_ROWS + hr, :] = z
        conv = (cw[0:1] * zbuf[hf, F32_ROWS - 2:F32_ROWS - 2 + hr, :]
                + cw[1:2] * zbuf[hf, F32_ROWS - 1:F32_ROWS - 1 + hr, :] + cw[2:3] * z)
        y_ref[:, 0:GROUP_WIDTH] = (p_ref[:, C_CB:C_CB + GROUP_WIDTH] * conv).astype(BF16)
        zbuf[hf, 0:F32_ROWS, :] = zbuf[hf, hr:hr + F32_ROWS, :]

        sv = p_ref[:, C_SV:C_SV + GROUP_WIDTH]
        sv_hi, sv_lo = _split_bf16(sv, 2)
        dv = sv - (_dot(sv_hi, group_mean) + _dot(sv_lo, group_mean))
        var = _dot((dv * dv).astype(BF16), group_mean)
        vn = (dv * lax.rsqrt(var + EPS)) * sgug_ref[...]
        gate_logit = _dot(p_ref[:, C_AG:C_AG + LANES].astype(BF16), wgate_ref[...]) + bgate_ref[...]
        log_a = (jnp.minimum(gate_logit, 0.0) - jnp.log(1.0 + jnp.exp(-jnp.abs(gate_logit)))) * (
            1.0 / GLA_GATE_NORMALIZER)
        f = lb + (1.0 - lb) * jax.nn.sigmoid(p_ref[:, C_HF:C_HF + GROUP_WIDTH])
        log_f = jnp.log(f)

        o_gla, o_hgrn = [], []
        for i in range(hr // SUPER):
            r = slice(i * SUPER, (i + 1) * SUPER)
            mixed = _dot(ws, _head_rows(vn[r])) + sgub_ref[...]
            y_ref[r, GROUP_WIDTH:2 * GROUP_WIDTH] = (p_ref[r, C_SU:C_SU + GROUP_WIDTH] * mixed).astype(BF16)
            o_gla.append(_gated_linear_attention_step(
                p_ref[r, C_AQ:C_AQ + LANES] * (GLA_KEY_DIM ** -0.5), p_ref[r, C_AK:C_AK + LANES],
                p_ref[r, C_AV:C_AV + GROUP_WIDTH], log_a[r], st_gla.at[hf], ltri, masks))
            o_hgrn.append(_gated_linear_attention_step(
                p_ref[r, C_HQ:C_HQ + GROUP_WIDTH], 1.0 - f[r], p_ref[r, C_HI:C_HI + GROUP_WIDTH], log_f[r],
                st_hgrn.at[hf], ltri, masks))
        y_ref[:, 2 * GROUP_WIDTH:3 * GROUP_WIDTH] = _head_rms_gate(
            jnp.concatenate(o_gla, axis=0), head_mean, glag_ref[...], p_ref[:, C_AR:C_AR + GROUP_WIDTH]).astype(BF16)
        y_ref[:, 3 * GROUP_WIDTH:4 * GROUP_WIDTH] = _head_rms_gate(
            jnp.concatenate(o_hgrn, axis=0), head_mean, hgng_ref[...], p_ref[:, C_HG:C_HG + GROUP_WIDTH]).astype(BF16)

        o_ref[hf] = x_ref[hf] + mod_ref[hf][2:3] * _dot(y_ref[...], wout_ref[...])


def _mixer_call(layer, x, mod, ng, win, wout, convw, sgug, sguw, sgub, wgate, bgate, glag, hlb, hgng):
    bsz, seq, d = x.shape
    hr = MIXER_ROWS
    assert bsz % 2 == 0 and seq % hr == 0
    small = [convw, sgug, sguw, sgub, wgate, bgate, glag, hlb, hgng]
    return pl.pallas_call(
        functools.partial(_mixer_kernel, layer, hr),
        grid=(bsz // 2, seq // hr),
        in_specs=[pl.BlockSpec((2, hr, d), lambda b, s: (b, s, 0)),
                  pl.BlockSpec((None, 2, N_MOD, d), lambda b, s: (layer, b, 0, 0)), _const_spec(ng.shape),
                  _layer_spec(win, layer), _layer_spec(wout, layer)] + [_const_spec(a.shape) for a in small],
        out_specs=pl.BlockSpec((2, hr, d), lambda b, s: (b, s, 0)),
        out_shape=jax.ShapeDtypeStruct(x.shape, F32),
        scratch_shapes=[pltpu.VMEM((hr, IN_PROJ_PAD), F32), pltpu.VMEM((hr, IN_PROJ_PAD), F32),
                        pltpu.VMEM((hr, d), BF16), pltpu.VMEM((hr, d), BF16),
                        pltpu.VMEM((2, hr + F32_ROWS, GROUP_WIDTH), F32),
                        pltpu.VMEM((2, GROUP_WIDTH, HEADS * GLA_KEY_DIM), F32),
                        pltpu.VMEM((2, GROUP_WIDTH, GROUP_WIDTH), F32)],
        input_output_aliases={0: 0} if layer > 0 else {},
        compiler_params=pltpu.CompilerParams(dimension_semantics=("parallel", "arbitrary"),
                                             vmem_limit_bytes=VMEM_LIMIT),
        name=f"mixer_l{layer}",
    )(x, mod, ng, win, wout, *small)


def _final_norm(out, g):
    ms = jnp.mean(out * out, axis=-1, keepdims=True)
    return (out * lax.rsqrt(ms + EPS)) * g


def _ffn_kernel(final, x_ref, mod_ref, ng_ref, w1_ref, w3_ref, w2_ref, fng_ref, o_ref):
    xt = x_ref[0]
    mod = mod_ref[...]
    hb = _mod_norm(xt, ng_ref[...], mod[3:4], mod[4:5]).astype(BF16)
    h1 = _dot(hb, w1_ref[...])
    h3 = _dot(hb, w3_ref[...])
    a = (_silu(h1) * h3).astype(BF16)
    out = xt + mod[5:6] * _dot(a, w2_ref[...])
    o_ref[0] = _final_norm(out, fng_ref[...]) if final else out


def _ffn_call(final, layer, x, mod, ng, w1, w3, w2, fng):
    bsz, seq, d = x.shape
    tm = FFN_ROWS
    idx = layer // 2
    return pl.pallas_call(
        functools.partial(_ffn_kernel, final),
        grid=(bsz, seq // tm),
        in_specs=[pl.BlockSpec((1, tm, d), lambda b, s: (b, s, 0)), _mod_spec(layer, d), _const_spec(ng.shape),
                  _layer_spec(w1, idx), _layer_spec(w3, idx), _layer_spec(w2, idx), _const_spec(fng.shape)],
        out_specs=pl.BlockSpec((1, tm, d), lambda b, s: (b, s, 0)),
        out_shape=jax.ShapeDtypeStruct(x.shape, F32),
        input_output_aliases={0: 0},
        compiler_params=pltpu.CompilerParams(dimension_semantics=("parallel", "arbitrary"),
                                             vmem_limit_bytes=VMEM_LIMIT),
        name="dense_ffn",
    )(x, mod, ng, w1, w3, w2, fng)


SEG_SIZES = tuple(MOE_TB >> i for i in range((MOE_TB // BF16_ROWS).bit_length()))


def _segment_copies(fn, buf_off, hbm_off, nrows):
    pos = jnp.int32(0)
    for size in SEG_SIZES:
        fn(pl.multiple_of(buf_off + pos, BF16_ROWS), pl.multiple_of(hbm_off + pos, BF16_ROWS), size,
           (nrows & size) != 0)
        pos = pos + (nrows & size)


def _route_kernel(nsb, nblk, x_ref, mod_ref, ng_ref, wr_ref, sut_ref, xs_hbm, route_ref, seg_ref,
                  xs_buf, carry_buf, run_ref, prev_ref, sem):
    tb, slots = MOE_TB, MOE_SLOTS
    blk = pl.program_id(0) * nsb + pl.program_id(1)

    @pl.when(blk == 0)
    def _():
        carry_buf[...] = jnp.zeros_like(carry_buf)
        for e in range(N_EXPERTS):
            run_ref[e] = 0
        for i in range(3 * N_EXPERTS):
            prev_ref[i] = 0

    mod = mod_ref[...]
    h = _mod_norm(x_ref[0], ng_ref[...], mod[3:4], mod[4:5])
    h_hi, h_lo = _split_bf16(h, 2)
    lg = _dot(h_hi, wr_ref[...]) + _dot(h_lo, wr_ref[...])
    logits = lg[:, 0:LANES] + lg[:, LANES:2 * LANES]
    lt = logits.T[0:N_EXPERTS, :]
    eid = lax.broadcasted_iota(I32, (N_EXPERTS, tb), 0)
    m1 = jnp.max(lt, axis=0, keepdims=True)
    i1 = jnp.min(jnp.where(lt == m1, eid, N_EXPERTS), axis=0, keepdims=True)
    lt2 = jnp.where(eid == i1, -jnp.inf, lt)
    m2 = jnp.max(lt2, axis=0, keepdims=True)
    i2 = jnp.min(jnp.where(lt2 == m2, eid, N_EXPERTS), axis=0, keepdims=True)
    e2 = jnp.exp(m2 - m1)
    w_first = 1.0 / (1.0 + e2)
    w_second = e2 / (1.0 + e2)

    sel1, sel2 = eid == i1, eid == i2
    onehot = jnp.where(sel1 | sel2, 1.0, 0.0).astype(F32)
    rank = _dot(onehot.astype(BF16), sut_ref[...])
    counts = [jnp.sum(onehot[e:e + 1, :]).astype(I32) for e in range(N_EXPERTS)]
    runs = [run_ref[e] for e in range(N_EXPERTS)]
    rem = [r & (BF16_ROWS - 1) for r in runs]
    base = [r - m for r, m in zip(runs, rem)]
    width = [m + n for m, n in zip(rem, counts)]
    win16 = [((w + (BF16_ROWS - 1)) >> BF16_ROWS_LOG2) << BF16_ROWS_LOG2 for w in width]
    nfull = [(w >> BF16_ROWS_LOG2) << BF16_ROWS_LOG2 for w in width]
    left = [w - f for w, f in zip(width, nfull)]
    offs, total = [], jnp.int32(0)
    for e in range(N_EXPERTS):
        offs.append(total)
        total = total + win16[e]

    sub = lax.broadcasted_iota(I32, (N_EXPERTS, 1), 0)

    def expert_column(vals):
        col = jnp.zeros((N_EXPERTS, 1), I32)
        for e in range(N_EXPERTS):
            col = jnp.where(sub == e, vals[e], col)
        return col

    off_col, win_col, base_col = expert_column(offs), expert_column(win16), expert_column(base)
    first_col = expert_column([o + m for o, m in zip(offs, rem)])
    slot_all = first_col.astype(F32) + rank
    slot1 = jnp.sum(jnp.where(sel1, slot_all, 0.0), axis=0, keepdims=True)
    slot2 = jnp.sum(jnp.where(sel2, slot_all, 0.0), axis=0, keepdims=True)

    rowi = lax.broadcasted_iota(I32, (slots, tb), 0)
    perm = jnp.where((rowi == slot1.astype(I32)) | (rowi == slot2.astype(I32)), 1.0, 0.0).astype(BF16)
    xs_sorted = _dot(perm, h_hi).astype(BF16)

    def copy(e, buf_row, hbm_row, size):
        return pltpu.make_async_copy(xs_buf.at[pl.ds(buf_row, size), :], xs_hbm.at[e, pl.ds(hbm_row, size), :], sem)

    def start(e):
        def fn(buf_row, hbm_row, size, cond):
            @pl.when(cond)
            def _():
                copy(e, buf_row, hbm_row, size).start()
        return fn

    def wait(e):
        def fn(buf_row, hbm_row, size, cond):
            @pl.when(cond)
            def _():
                copy(e, buf_row, hbm_row, size).wait()
        return fn

    for e in range(N_EXPERTS):
        _segment_copies(wait(e), prev_ref[e], prev_ref[2 * N_EXPERTS + e], prev_ref[N_EXPERTS + e])
    xs_buf[...] = xs_sorted
    for e in range(N_EXPERTS):
        head = pl.ds(pl.multiple_of(offs[e], BF16_ROWS), BF16_ROWS)
        xs_buf[head, :] = xs_buf[head, :] + carry_buf[e]
    for e in range(N_EXPERTS):
        _segment_copies(start(e), offs[e], base[e], nfull[e])
    for e in range(N_EXPERTS):
        tail = xs_buf[pl.ds(pl.multiple_of(offs[e] + nfull[e], BF16_ROWS), BF16_ROWS), :]
        carry_buf[e] = jnp.where(left[e] > 0, tail, jnp.zeros_like(tail))

    @pl.when(blk == nblk - 1)
    def _():
        for e in range(N_EXPERTS):
            _segment_copies(wait(e), offs[e], base[e], nfull[e])
        for e in range(N_EXPERTS):
            @pl.when(left[e] > 0)
            def _(e=e):
                flush = pltpu.make_async_copy(
                    carry_buf.at[e],
                    xs_hbm.at[e, pl.ds(pl.multiple_of(base[e] + nfull[e], BF16_ROWS), BF16_ROWS), :], sem)
                flush.start()
                flush.wait()

    route_ref[0] = jnp.concatenate([slot1, slot2, w_first, w_second, jnp.zeros((4, tb), F32)], axis=0)
    lane = lax.broadcasted_iota(I32, (N_EXPERTS, LANES), 1)
    after_col = expert_column([r + n for r, n in zip(runs, counts)])
    seg_ref[0] = jnp.where(lane == 0, off_col, jnp.where(lane == 1, win_col, jnp.where(
        lane == 2, base_col, jnp.where(lane == 3, after_col, 0))))
    for e in range(N_EXPERTS):
        run_ref[e] = runs[e] + counts[e]
        prev_ref[e] = offs[e]
        prev_ref[N_EXPERTS + e] = nfull[e]
        prev_ref[2 * N_EXPERTS + e] = base[e]


def _route_call(layer, x, mod, ng, wr2, cap):
    bsz, seq, d = x.shape
    tb = MOE_TB
    nsb = seq // tb
    nblk = bsz * nsb
    ri = lax.broadcasted_iota(I32, (tb, tb), 0)
    ci = lax.broadcasted_iota(I32, (tb, tb), 1)
    sut = (ri < ci).astype(BF16)
    return pl.pallas_call(
        functools.partial(_route_kernel, nsb, nblk),
        grid=(bsz, nsb),
        in_specs=[pl.BlockSpec((1, tb, d), lambda b, s: (b, s, 0)), _mod_spec(layer, d),
                  _const_spec(ng.shape), _const_spec(wr2.shape), _const_spec((tb, tb))],
        out_specs=[pl.BlockSpec(memory_space=pl.ANY),
                   pl.BlockSpec((1, 8, tb), lambda b, s: (b * nsb + s, 0, 0)),
                   pl.BlockSpec((1, N_EXPERTS, LANES), lambda b, s: (b * nsb + s, 0, 0))],
        out_shape=[jax.ShapeDtypeStruct((N_EXPERTS, cap, d), BF16),
                   jax.ShapeDtypeStruct((nblk, 8, tb), F32),
                   jax.ShapeDtypeStruct((nblk, N_EXPERTS, LANES), I32)],
        scratch_shapes=[pltpu.VMEM((MOE_SLOTS, d), BF16), pltpu.VMEM((N_EXPERTS, BF16_ROWS, d), BF16),
                        pltpu.SMEM((N_EXPERTS,), I32), pltpu.SMEM((3 * N_EXPERTS,), I32), pltpu.SemaphoreType.DMA],
        compiler_params=pltpu.CompilerParams(dimension_semantics=("arbitrary", "arbitrary"),
                                             vmem_limit_bytes=VMEM_LIMIT),
        name="moe_route",
    )(x, mod, ng, wr2, sut)


def _gmm_kernel(idx, te_ref, trb_ref, tval_ref, tnv_ref, xs_ref, w1_hbm, w3_hbm, w2_hbm, ys_ref,
                w1_ref, w3_ref, w2_ref, stage_in, stage_out, sem):
    i = pl.program_id(0)
    e = te_ref[i]
    tf = MOE_FF_CHUNK
    nf = w1_ref.shape[-1] // tf

    def chunk_copies(j, slot):
        return (pltpu.make_async_copy(w1_hbm.at[idx, e, :, pl.ds(j * tf, tf)], stage_in.at[slot, 0], sem.at[slot, 0]),
                pltpu.make_async_copy(w3_hbm.at[idx, e, :, pl.ds(j * tf, tf)], stage_in.at[slot, 1], sem.at[slot, 1]),
                pltpu.make_async_copy(w2_hbm.at[idx, e, pl.ds(j * tf, tf), :], stage_out.at[slot], sem.at[slot, 2]))

    def tile(load):
        tm = xs_ref.shape[0]
        rows = lax.broadcasted_iota(I32, (tm, 1), 0)
        xs = xs_ref[...]
        xs = jnp.where(rows < tnv_ref[i], xs, jnp.zeros_like(xs))
        if load:
            for cp in chunk_copies(0, 0):
                cp.start()
        acc = None
        for j in range(nf):
            cols = slice(j * tf, (j + 1) * tf)
            if load:
                slot = j % 2
                if j + 1 < nf:
                    for cp in chunk_copies(j + 1, 1 - slot):
                        cp.start()
                for cp in chunk_copies(j, slot):
                    cp.wait()
                w1_ref[:, cols] = stage_in[slot, 0].astype(BF16)
                w3_ref[:, cols] = stage_in[slot, 1].astype(BF16)
                w2_ref[cols, :] = stage_out[slot].astype(BF16)
            h1 = _dot(xs, w1_ref[:, cols])
            h3 = _dot(xs, w3_ref[:, cols])
            part = _dot((_silu(h1) * h3).astype(BF16), w2_ref[cols, :])
            acc = part if acc is None else acc + part
        ys_ref[...] = acc.astype(BF16)

    valid = tval_ref[i] == 1
    first = (i == 0) | (e != te_ref[jnp.maximum(i - 1, 0)])

    @pl.when(valid & first)
    def _():
        tile(True)

    @pl.when(valid & jnp.logical_not(first))
    def _():
        tile(False)


def _gmm_call(idx, xs, w1, w3, w2, te, trb, tval, tnv):
    ne, cap, d = xs.shape
    dff = w1.shape[-1]
    tm = MOE_TILE_ROWS
    nt = te.shape[0]
    anywhere = pl.BlockSpec(memory_space=pl.ANY)

    return pl.pallas_call(
        functools.partial(_gmm_kernel, idx),
        grid_spec=pltpu.PrefetchScalarGridSpec(
            num_scalar_prefetch=4,
            grid=(nt,),
            in_specs=[pl.BlockSpec((None, tm, d), lambda i, te, trb, tval, tnv: (te[i], trb[i], 0)),
                      anywhere, anywhere, anywhere],
            out_specs=pl.BlockSpec((None, tm, d), lambda i, te, trb, tval, tnv: (te[i], trb[i], 0)),
            scratch_shapes=[pltpu.VMEM((d, dff), BF16), pltpu.VMEM((d, dff), BF16), pltpu.VMEM((dff, d), BF16),
                            pltpu.VMEM((2, 2, d, MOE_FF_CHUNK), F32), pltpu.VMEM((2, MOE_FF_CHUNK, d), F32),
                            pltpu.SemaphoreType.DMA((2, 3))]),
        out_shape=jax.ShapeDtypeStruct((ne, cap, d), BF16),
        compiler_params=pltpu.CompilerParams(dimension_semantics=("arbitrary",), vmem_limit_bytes=VMEM_LIMIT),
        name="moe_experts",
    )(te, trb, tval, tnv, xs, w1, w3, w2)


def _combine_kernel(final, nsb, nblk, seg_ref, x_ref, mod_ref, route_ref, fng_ref, ys_hbm, o_ref, ys_buf, sem):
    tb, slots = MOE_TB, MOE_SLOTS
    blk = pl.program_id(0) * nsb + pl.program_id(1)
    cur = blk & 1

    def segments(b):
        base = b * (3 * N_EXPERTS)
        return ([seg_ref[base + e] for e in range(N_EXPERTS)],
                [seg_ref[base + N_EXPERTS + e] for e in range(N_EXPERTS)],
                [seg_ref[base + 2 * N_EXPERTS + e] for e in range(N_EXPERTS)])

    def fetch(b, buf, also, wait):
        offs, cnt16, runs = segments(b)
        for e in range(N_EXPERTS):
            def fn(buf_row, hbm_row, size, cond, e=e):
                @pl.when(cond & also)
                def _():
                    cp = pltpu.make_async_copy(ys_hbm.at[e, pl.ds(hbm_row, size), :],
                                               ys_buf.at[buf, pl.ds(buf_row, size), :], sem.at[buf])
                    cp.wait() if wait else cp.start()
            _segment_copies(fn, offs[e], runs[e], cnt16[e])
        return offs[N_EXPERTS - 1] + cnt16[N_EXPERTS - 1]

    @pl.when(blk == 0)
    def _():
        fetch(blk, cur, True, wait=False)

    fetch(jnp.minimum(blk + 1, nblk - 1), 1 - cur, blk + 1 < nblk, wait=False)
    total = fetch(blk, cur, True, wait=True)

    rowi = lax.broadcasted_iota(I32, (slots, 1), 0)
    ys = ys_buf[cur]
    ys = jnp.where(rowi < total, ys, jnp.zeros_like(ys))

    route = jnp.concatenate([route_ref[0], jnp.zeros((LANES - 8, tb), F32)], axis=0).T
    slot1, slot2 = route[:, 0:1].astype(I32), route[:, 1:2].astype(I32)
    lane = lax.broadcasted_iota(I32, (tb, slots), 1)
    comb = (jnp.where(lane == slot1, route[:, 2:3], 0.0) + jnp.where(lane == slot2, route[:, 3:4], 0.0)).astype(BF16)
    out = x_ref[0] + mod_ref[...][5:6] * _dot(comb, ys)
    o_ref[0] = _final_norm(out, fng_ref[...]) if final else out


def _combine_call(final, layer, x, mod, route, seg_flat, ys, fng):
    bsz, seq, d = x.shape
    tb = MOE_TB
    nsb = seq // tb
    return pl.pallas_call(
        functools.partial(_combine_kernel, final, nsb, bsz * nsb),
        grid_spec=pltpu.PrefetchScalarGridSpec(
            num_scalar_prefetch=1,
            grid=(bsz, nsb),
            in_specs=[pl.BlockSpec((1, tb, d), lambda b, s, seg: (b, s, 0)), _mod_spec(layer, d),
                      pl.BlockSpec((1, 8, tb), lambda b, s, seg: (b * nsb + s, 0, 0)),
                      _const_spec(fng.shape),
                      pl.BlockSpec(memory_space=pl.ANY)],
            out_specs=pl.BlockSpec((1, tb, d), lambda b, s, seg: (b, s, 0)),
            scratch_shapes=[pltpu.VMEM((2, MOE_SLOTS, d), BF16), pltpu.SemaphoreType.DMA((2,))]),
        out_shape=jax.ShapeDtypeStruct(x.shape, F32),
        input_output_aliases={1: 0},
        compiler_params=pltpu.CompilerParams(dimension_semantics=("arbitrary", "arbitrary"),
                                             vmem_limit_bytes=VMEM_LIMIT),
        name="moe_combine",
    )(seg_flat, x, mod, route, fng, ys)


def _moe_tiles(seg, tm, nt):
    total = seg[-1, :, 3]
    tiles = (total + tm - 1) // tm
    ends = jnp.cumsum(tiles)
    starts = ends - tiles
    ntiles = ends[-1]
    i = jnp.arange(nt, dtype=I32)
    valid = i < ntiles
    ie = jnp.minimum(i, ntiles - 1)
    te = jnp.minimum(jnp.sum((ie[:, None] >= ends[None, :]).astype(I32), axis=1), N_EXPERTS - 1)
    trb = ie - starts[te]
    tnv = jnp.clip(total[te] - trb * tm, 0, tm)
    return te.astype(I32), trb.astype(I32), valid.astype(I32), tnv.astype(I32)


def _moe_layer(final, layer, x, mod, ng, wr2, w1, w3, w2, fng):
    bsz, seq, d = x.shape
    tokens = bsz * seq
    nblk = tokens // MOE_TB
    tm = MOE_TILE_ROWS
    cap = -(-(tokens + BF16_ROWS) // tm) * tm
    nt = -(-2 * tokens // tm) + N_EXPERTS
    xs, route, seg = _route_call(layer, x, mod, ng, wr2, cap)
    te, trb, tval, tnv = _moe_tiles(seg, tm, nt)
    ys = _gmm_call(layer // 2, xs, w1, w3, w2, te, trb, tval, tnv)
    seg_flat = jnp.transpose(seg[:, :, 0:3], (0, 2, 1)).reshape(-1)
    return _combine_call(final, layer, x, mod, route, seg_flat, ys, fng)


def kernel(x, c, norm_mix_g, norm_ffn_g, final_norm_g, w_ada, b_ada, w_in, w_out, conv_w, sgu_norm_g, sgu_w, sgu_b,
           gla_w_gate, gla_b_gate, gla_norm_g, hgrn_lower_bounds, hgrn_norm_g, ffn_w1, ffn_w3, ffn_w2, moe_router,
           moe_w1, moe_w3, moe_w2):
    depth = w_in.shape[0]
    bsz, seq, d = x.shape
    assert d == D_MODEL and seq % MOE_TB == 0 and hgrn_lower_bounds.shape[0] == depth

    mod = _ada_call(c, w_ada, b_ada).reshape(depth, bsz, N_MOD, d)

    win = jnp.concatenate([w_in[:, :, :IN_PROJ_SPLIT],
                           jnp.zeros((depth, d, LANES - GLA_GATE_RANK), w_in.dtype),
                           w_in[:, :, IN_PROJ_SPLIT:]], axis=-1).astype(BF16)
    wout = w_out.astype(BF16)
    wgate = jnp.concatenate([gla_w_gate, jnp.zeros((depth, LANES - GLA_GATE_RANK, gla_w_gate.shape[-1]), F32)],
                            axis=1).astype(BF16)
    fng = final_norm_g.reshape(1, d)
    ffn_w = [w.astype(BF16) for w in (ffn_w1, ffn_w3, ffn_w2)]
    moe_w = (moe_w1, moe_w3, moe_w2)
    wr = jnp.concatenate([moe_router, jnp.zeros(moe_router.shape[:2] + (LANES - N_EXPERTS,), F32)], axis=-1)
    wr_hi = wr.astype(BF16)
    wr2 = jnp.concatenate([wr_hi, (wr - wr_hi.astype(F32)).astype(BF16)], axis=-1)

    for layer in range(depth):
        x = _mixer_call(
            layer, x, mod, norm_mix_g[layer].reshape(1, d), win, wout, conv_w[layer],
            sgu_norm_g[layer].reshape(1, GROUP_WIDTH),
            jnp.transpose(sgu_w[layer], (1, 0, 2)).reshape(SUPER, HEADS * SUPER),
            jnp.repeat(sgu_b[layer].T, HEAD_DIM, axis=1), wgate[layer], gla_b_gate[layer].reshape(1, -1),
            jnp.tile(gla_norm_g[layer], HEADS).reshape(1, GROUP_WIDTH), hgrn_lower_bounds,
            jnp.tile(hgrn_norm_g[layer], HEADS).reshape(1, GROUP_WIDTH))
        final = layer == depth - 1
        ng = norm_ffn_g[layer].reshape(1, d)
        if layer % 2 == 0:
            x = _ffn_call(final, layer, x, mod, ng, *ffn_w, fng)
        else:
            x = _moe_layer(final, layer, x, mod, ng, wr2[layer // 2], *moe_w, fng)
    return x
```
